```python
import math
import jax, jax.numpy as jnp
from jax import lax
import numpy as np

D_MODEL = 2048
BATCH = 8
SEQ = 4096
DEPTH = 4
DEC_BATCH = 32
DEC_SEQ = 16
PAST_LEN = 4096

CHUNK = 64
N_AB = (DEPTH + 1) // 2
N_C = DEPTH // 2
SSD_HEAD_DIM = 64
SSD_INNER = D_MODEL
SSD_HEADS = SSD_INNER // SSD_HEAD_DIM
SSD_GROUPS = 8
SSD_STATE = 128
SSD_CONV = 4
SSD_CONV_DIM = SSD_INNER + 2 * SSD_GROUPS * SSD_STATE
SSD_BLOCK = CHUNK
HG_KEY = 128
HG_VAL = 128
HG_HEADS = D_MODEL // HG_KEY
HG_DIM = HG_HEADS * HG_KEY
HG_BLOCK = 16
POOL_WINDOWS = (2, 4, 8, 16)
POOL_GROUPS = len(POOL_WINDOWS)
POOL_GROUP_DIM = D_MODEL // POOL_GROUPS
POOL_BUF = max(POOL_WINDOWS) - 1
D_FF = 4 * D_MODEL
EPS = 1e-6
IN_SIZES = (SSD_INNER, SSD_CONV_DIM, SSD_HEADS, HG_DIM, HG_DIM, HG_DIM, HG_DIM)
IN_DIM = sum(IN_SIZES)
IN_SPLITS = tuple(int(v) for v in np.cumsum(IN_SIZES)[:-1])
MIX_DIM = SSD_INNER + HG_DIM

kernel_name = 'hybrid_ssd_hgrn2_pool_stream_step'


def _rms_norm(x, w):
    xf = x.astype(jnp.float32)
    y = xf * lax.rsqrt(jnp.mean(xf * xf, axis=-1, keepdims=True) + EPS)
    return (y * w.astype(jnp.float32)).astype(x.dtype)


def _block_len(t, pref):
    return pref if t % pref == 0 else t


def _to_blocks(t, nb, bl):
    return jnp.moveaxis(t.reshape((t.shape[0], nb, bl) + t.shape[2:]), 1, 0)


def _causal_conv(u, buf, w, b):
    T = u.shape[1]
    up = jnp.concatenate([buf.astype(u.dtype), u], axis=1)
    out = sum(up[:, k:k + T] * w[k].astype(u.dtype) for k in range(SSD_CONV)) + b.astype(u.dtype)
    return out, up[:, T:]


def _ssd_scan(x, dt, a_log, bm, cm, h0):
    b, T, H, P = x.shape
    G, N = bm.shape[2], bm.shape[3]
    R = H // G
    L = _block_len(T, SSD_BLOCK)
    nb = T // L
    f32 = jnp.float32
    a = dt * (-jnp.exp(a_log.astype(f32)))
    xs = _to_blocks(x.astype(f32).reshape(b, T, G, R, P), nb, L)
    dts = _to_blocks(dt.reshape(b, T, G, R), nb, L)
    as_ = _to_blocks(a.reshape(b, T, G, R), nb, L)
    bs = _to_blocks(bm.astype(f32), nb, L)
    cs = _to_blocks(cm.astype(f32), nb, L)
    causal = jnp.tril(jnp.ones((L, L), bool))[None, :, :, None, None]

    def step(h, inp):
        xc, dtc, ac, bc, cc = inp
        acs = jnp.cumsum(ac, axis=1)
        decay = jnp.exp(jnp.where(causal, acs[:, :, None] - acs[:, None], -jnp.inf))
        cb = jnp.einsum('blgn,bsgn->blsg', cc, bc)
        dx = dtc[..., None] * xc
        y = jnp.einsum('blsgr,bsgrp->blgrp', cb[..., None] * decay, dx)
        y = y + jnp.einsum('blgn,bgrpn->blgrp', cc, h) * jnp.exp(acs)[..., None]
        tail = jnp.exp(acs[:, -1:] - acs)[..., None] * dx
        h = h * jnp.exp(acs[:, -1])[..., None, None] + jnp.einsum('blgn,blgrp->bgrpn', bc, tail)
        return h, y

    h, ys = lax.scan(step, h0.astype(f32).reshape(b, G, R, P, N), (xs, dts, as_, bs, cs))
    return jnp.moveaxis(ys, 0, 1).reshape(b, T, H, P), h.reshape(b, H, P, N)


def _hgrn_scan(q, logf, v, s0):
    b, T, H, K = q.shape
    L = _block_len(T, HG_BLOCK)
    nb = T // L
    f32 = jnp.float32
    k = -jnp.expm1(logf)
    qs = _to_blocks(q.astype(f32), nb, L)
    ls = _to_blocks(logf, nb, L)
    ks = _to_blocks(k, nb, L)
    vs = _to_blocks(v.astype(f32), nb, L)
    causal = jnp.tril(jnp.ones((L, L), bool))[None, :, :, None, None]

    def step(s, inp):
        qc, lc, kc, vc = inp
        bcs = jnp.cumsum(lc, axis=1)
        w = jnp.exp(jnp.where(causal, bcs[:, :, None] - bcs[:, None], -jnp.inf))
        att = jnp.einsum('blshk,bshk->blsh', w * qc[:, :, None], kc)
        o = jnp.einsum('blsh,bshv->blhv', att, vc)
        o = o + jnp.einsum('blhk,bhkv->blhv', qc * jnp.exp(bcs), s)
        tail = kc * jnp.exp(bcs[:, -1:] - bcs)
        s = s * jnp.exp(bcs[:, -1])[..., None] + jnp.einsum('blhk,blhv->bhkv', tail, vc)
        return s, o

    s, os_ = lax.scan(step, s0.astype(f32), (qs, ls, ks, vs))
    return jnp.moveaxis(os_, 0, 1).reshape(b, T, H, v.shape[-1]), s


def _mix_ssd_hgrn(h, conv_buf, ssd_h, hg_s, w_in, conv_w, conv_b, dt_bias, a_log, d_skip,
                  ssd_norm_w, hg_norm_w, lb, w_out):
    b, T, _ = h.shape
    f32 = jnp.float32
    proj = h @ w_in.astype(h.dtype)
    z, xbc, dt_raw, q, fz, iv, g = jnp.split(proj, IN_SPLITS, axis=-1)
    xbc, new_conv = _causal_conv(xbc, conv_buf, conv_w, conv_b)
    xbc = jax.nn.silu(xbc)
    xs, bm, cm = jnp.split(xbc, [SSD_INNER, SSD_INNER + SSD_GROUPS * SSD_STATE], axis=-1)
    xs = xs.reshape(b, T, SSD_HEADS, SSD_HEAD_DIM)
    dt = jax.nn.softplus(dt_raw.astype(f32) + dt_bias.astype(f32))
    y, new_ssd = _ssd_scan(xs, dt, a_log, bm.reshape(b, T, SSD_GROUPS, SSD_STATE),
                           cm.reshape(b, T, SSD_GROUPS, SSD_STATE), ssd_h)
    y = y + d_skip.astype(f32)[:, None] * xs.astype(f32)
    y = y.reshape(b, T, SSD_INNER) * jax.nn.silu(z.astype(f32))
    y = y.reshape(b, T, SSD_GROUPS, SSD_INNER // SSD_GROUPS)
    y = y * lax.rsqrt(jnp.mean(y * y, axis=-1, keepdims=True) + EPS)
    y = y.reshape(b, T, SSD_INNER) * ssd_norm_w.astype(f32)
    lbf = lb.reshape(HG_HEADS, HG_KEY)
    logf = jnp.logaddexp(jnp.log(lbf), jnp.log1p(-lbf) +
                         jax.nn.log_sigmoid(fz.astype(f32).reshape(b, T, HG_HEADS, HG_KEY)))
    o, new_hg = _hgrn_scan(q.reshape(b, T, HG_HEADS, HG_KEY), logf,
                           iv.reshape(b, T, HG_HEADS, HG_VAL), hg_s)
    o = o * lax.rsqrt(jnp.mean(o * o, axis=-1, keepdims=True) + EPS)
    o = o.reshape(b, T, HG_DIM) * hg_norm_w.astype(f32) * jax.nn.silu(g.astype(f32))
    mixed = jnp.concatenate([y, o], axis=-1).astype(h.dtype) @ w_out.astype(h.dtype)
    return (mixed, new_conv.astype(conv_buf.dtype), new_ssd.astype(ssd_h.dtype),
            new_hg.astype(hg_s.dtype))


def _mix_pool(h, buf, pos0, pool_w, pool_scale):
    b, T, D = h.shape
    f32 = jnp.float32
    u = jnp.concatenate([buf.astype(h.dtype), h], axis=1).astype(f32)
    cs = jnp.concatenate([jnp.zeros((b, 1, D), f32), jnp.cumsum(u, axis=1)], axis=1)
    end = cs[:, POOL_BUF + 1:]
    pos = pos0 + jnp.arange(T)
    parts = []
    for gi, w in enumerate(POOL_WINDOWS):
        sl = slice(gi * POOL_GROUP_DIM, (gi + 1) * POOL_GROUP_DIM)
        start = cs[:, POOL_BUF + 1 - w:POOL_BUF + 1 - w + T, sl]
        cnt = jnp.minimum(pos + 1, w).astype(f32)[None, :, None]
        parts.append((end[..., sl] - start) / cnt)
    pooled = jnp.concatenate(parts, axis=-1) - h.astype(f32)
    mixed = jnp.einsum('btgc,gcd->btgd', pooled.reshape(b, T, POOL_GROUPS, POOL_GROUP_DIM),
                       pool_w.astype(f32)).reshape(b, T, D) * pool_scale.astype(f32)
    return mixed.astype(h.dtype), u[:, T:].astype(buf.dtype)


def _trunk(x, conv_st, ssd_st, hg_st, pool_st, pos0, norm_w, w_in, conv_w, conv_b, dt_bias,
           a_log, d_skip, ssd_norm_w, hg_norm_w, lbs, w_out, pool_w, pool_scale,
           w_ffn_up, w_ffn_down):
    new_conv, new_ssd, new_hg, new_pool = [], [], [], []
    for layer in range(DEPTH):
        j = layer // 2
        hn = _rms_norm(x, norm_w[layer, 0])
        if layer % 2 == 0:
            mixed, c, s, gs = _mix_ssd_hgrn(hn, conv_st[j], ssd_st[j], hg_st[j], w_in[j], conv_w[j],
                                            conv_b[j], dt_bias[j], a_log[j], d_skip[j],
                                            ssd_norm_w[j], hg_norm_w[j], lbs[j], w_out[j])
            new_conv.append(c)
            new_ssd.append(s)
            new_hg.append(gs)
        else:
            mixed, p = _mix_pool(hn, pool_st[j], pos0, pool_w[j], pool_scale[j])
            new_pool.append(p)
        x = x + _rms_norm(mixed, norm_w[layer, 1])
        hn = _rms_norm(x, norm_w[layer, 2])
        ff = jnp.square(jax.nn.relu(hn @ w_ffn_up[layer].astype(hn.dtype))) @ w_ffn_down[layer].astype(hn.dtype)
        x = x + _rms_norm(ff, norm_w[layer, 3])
    return x, jnp.stack(new_conv), jnp.stack(new_ssd), jnp.stack(new_hg), jnp.stack(new_pool)


def setup_inputs(seed: int = 0) -> dict:
    key = jax.random.key(seed)
    ks = jax.random.split(key, 24)
    f32 = jnp.float32

    def nrm(k, shape, scale):
        return jax.random.normal(k, shape, f32) * scale

    dt0 = jnp.exp(jax.random.uniform(ks[10], (N_AB, SSD_HEADS), f32,
                                     minval=math.log(1e-3), maxval=math.log(1e-1)))
    return {
        'x_prompt': nrm(ks[0], (BATCH, SEQ, D_MODEL), 1.0),
        'x_sample': nrm(ks[1], (DEC_BATCH, DEC_SEQ, D_MODEL), 1.0),
        'state_conv': nrm(ks[2], (N_AB, DEC_BATCH, SSD_CONV - 1, SSD_CONV_DIM), 1.0),
        'state_ssd': nrm(ks[3], (N_AB, DEC_BATCH, SSD_HEADS, SSD_HEAD_DIM, SSD_STATE), 0.5),
        'state_hgrn': nrm(ks[4], (N_AB, DEC_BATCH, HG_HEADS, HG_KEY, HG_VAL), 0.5),
        'state_pool': nrm(ks[5], (N_C, DEC_BATCH, POOL_BUF, D_MODEL), 1.0),
        'norm_w': 1.0 + nrm(ks[6], (DEPTH, 4, D_MODEL), 0.05),
        'w_in': nrm(ks[7], (N_AB, D_MODEL, IN_DIM), D_MODEL ** -0.5),
        'conv_w': nrm(ks[8], (N_AB, SSD_CONV, SSD_CONV_DIM), SSD_CONV ** -0.5),
        'conv_b': nrm(ks[9], (N_AB, SSD_CONV_DIM), 0.02),
        'dt_bias': dt0 + jnp.log(-jnp.expm1(-dt0)),
        'a_log': jnp.log(jax.random.uniform(ks[11], (N_AB, SSD_HEADS), f32, minval=1.0, maxval=16.0)),
        'd_skip': 1.0 + nrm(ks[12], (N_AB, SSD_HEADS), 0.1),
        'ssd_norm_w': 1.0 + nrm(ks[13], (N_AB, SSD_INNER), 0.05),
        'hg_norm_w': 1.0 + nrm(ks[14], (N_AB, HG_DIM), 0.05),
        'hg_lower_bounds': nrm(ks[15], (N_AB, HG_DIM), 1.0),
        'w_out': nrm(ks[16], (N_AB, MIX_DIM, D_MODEL), MIX_DIM ** -0.5),
        'pool_w': nrm(ks[17], (N_C, POOL_GROUPS, POOL_GROUP_DIM, POOL_GROUP_DIM), POOL_GROUP_DIM ** -0.5),
        'pool_scale': 1.0 + nrm(ks[18], (N_C, D_MODEL), 0.1),
        'w_ffn_up': nrm(ks[19], (DEPTH, D_MODEL, D_FF), D_MODEL ** -0.5),
        'w_ffn_down': nrm(ks[20], (DEPTH, D_FF, D_MODEL), D_FF ** -0.5),
    }


def reference(x_prompt, x_sample, state_conv, state_ssd, state_hgrn, state_pool, norm_w, w_in,
              conv_w, conv_b, dt_bias, a_log, d_skip, ssd_norm_w, hg_norm_w, hg_lower_bounds,
              w_out, pool_w, pool_scale, w_ffn_up, w_ffn_down):
    lbs = jnp.cumsum(jax.nn.softmax(hg_lower_bounds.astype(jnp.float32), axis=0), axis=0)
    lbs = lbs - lbs[0]
    dtp = x_prompt.dtype
    zc = jnp.zeros((N_AB, BATCH, SSD_CONV - 1, SSD_CONV_DIM), dtp)
    zs = jnp.zeros((N_AB, BATCH, SSD_HEADS, SSD_HEAD_DIM, SSD_STATE), dtp)
    zh = jnp.zeros((N_AB, BATCH, HG_HEADS, HG_KEY, HG_VAL), dtp)
    zp = jnp.zeros((N_C, BATCH, POOL_BUF, D_MODEL), dtp)
    y_prompt, conv_p, ssd_p, hgrn_p, pool_p = _trunk(
        x_prompt, zc, zs, zh, zp, 0, norm_w, w_in, conv_w, conv_b, dt_bias, a_log, d_skip,
        ssd_norm_w, hg_norm_w, lbs, w_out, pool_w, pool_scale, w_ffn_up, w_ffn_down)
    y_sample, conv_s, ssd_s, hgrn_s, pool_s = _trunk(
        x_sample, state_conv, state_ssd, state_hgrn, state_pool, PAST_LEN, norm_w, w_in, conv_w,
        conv_b, dt_bias, a_log, d_skip, ssd_norm_w, hg_norm_w, lbs, w_out, pool_w, pool_scale,
        w_ffn_up, w_ffn_down)
    return (y_prompt, y_sample, conv_p, ssd_p, hgrn_p, pool_p, conv_s, ssd_s, hgrn_s, pool_s)
```

```python
import functools

import jax
import jax.numpy as jnp
from jax import lax
from jax.experimental import pallas as pl
from jax.experimental.pallas import tpu as pltpu

F32 = jnp.float32
BF16 = jnp.bfloat16
EPS = 1e-6
PAST_LEN = 4096
POOL_WINDOWS = (2, 4, 8, 16)
POOL_BUF = max(POOL_WINDOWS) - 1
SSD_STATE = 128
SSD_CONV = 4
HG_KEY = 128
LANES = 128
SUBLANES = 8
VMEM_LIMIT_BYTES = 56 * 1024 * 1024

_NT = (((1,), (1,)), ((), ()))
_TN = (((0,), (0,)), ((), ()))


def _tile(n, candidates):
    for c in candidates:
        if n % c == 0:
            return c
    raise ValueError(f"no tile for {n} in {candidates}")


def _params(*sem):
    return pltpu.CompilerParams(dimension_semantics=sem, vmem_limit_bytes=VMEM_LIMIT_BYTES)


def _softplus(x):
    return jnp.maximum(x, 0.0) + jnp.log1p(jnp.exp(-jnp.abs(x)))


def _silu(x):
    return x * jax.nn.sigmoid(x)


def _roll_rows(x, shift):
    return pltpu.roll(x, shift, 0)


def _norm_mm_kernel(x_ref, nw_ref, w_ref, o_ref, hn_ref, *, act):
    @pl.when(pl.program_id(1) == 0)
    def _():
        x = x_ref[...]
        ms = jnp.mean(x * x, axis=-1, keepdims=True)
        hn_ref[...] = (x * lax.rsqrt(ms + EPS) * nw_ref[...]).astype(BF16)

    acc = jnp.dot(hn_ref[...], w_ref[...], preferred_element_type=F32)
    if act == "relu2":
        acc = jnp.square(jnp.maximum(acc, 0.0))
    o_ref[...] = acc.astype(o_ref.dtype)


def _norm_mm(x, nw, w, *, act=None, out_dtype=F32, name):
    M, K = x.shape
    N = w.shape[1]
    tm = _tile(M, (512, 256, 128, 64, 32, 16))
    tn = _tile(N, (1024, 512, 256, 128))
    return pl.pallas_call(
        functools.partial(_norm_mm_kernel, act=act),
        grid=(M // tm, N // tn),
        in_specs=[
            pl.BlockSpec((tm, K), lambda i, j: (i, 0)),
            pl.BlockSpec((1, K), lambda i, j: (0, 0)),
            pl.BlockSpec((K, tn), lambda i, j: (0, j)),
        ],
        out_specs=pl.BlockSpec((tm, tn), lambda i, j: (i, j)),
        out_shape=jax.ShapeDtypeStruct((M, N), out_dtype),
        scratch_shapes=[pltpu.VMEM((tm, K), BF16)],
        compiler_params=_params("parallel", "arbitrary"),
        name=name,
    )(x, nw.reshape(1, K), w)


def _mm_res_norm_kernel(x_ref, w_ref, r_ref, nw_ref, o_ref, acc_ref, *, nj, tn):
    j = pl.program_id(1)
    acc_ref[j] = jnp.dot(x_ref[...], w_ref[...], preferred_element_type=F32)

    @pl.when(j == nj - 1)
    def _():
        ss = None
        for jj in range(nj):
            a = acc_ref[jj]
            s = jnp.sum(a * a, axis=-1, keepdims=True)
            ss = s if ss is None else ss + s
        rs = lax.rsqrt(ss / (nj * tn) + EPS)
        for jj in range(nj):
            sl = slice(jj * tn, (jj + 1) * tn)
            o_ref[:, sl] = r_ref[:, sl] + acc_ref[jj] * rs * nw_ref[:, sl]


def _mm_res_norm(x, w, resid, nw, *, name):
    M, K = x.shape
    N = w.shape[1]
    tm = _tile(M, (512, 256, 128, 64, 32, 16))
    tn = _tile(N, (256, 128)) if K > 4096 else _tile(N, (512, 256, 128))
    nj = N // tn
    return pl.pallas_call(
        functools.partial(_mm_res_norm_kernel, nj=nj, tn=tn),
        grid=(M // tm, nj),
        in_specs=[
            pl.BlockSpec((tm, K), lambda i, j: (i, 0)),
            pl.BlockSpec((K, tn), lambda i, j: (0, j)),
            pl.BlockSpec((tm, N), lambda i, j: (i, 0)),
            pl.BlockSpec((1, N), lambda i, j: (0, 0)),
        ],
        out_specs=pl.BlockSpec((tm, N), lambda i, j: (i, 0)),
        out_shape=jax.ShapeDtypeStruct((M, N), F32),
        scratch_shapes=[pltpu.VMEM((nj, tm, tn), F32)],
        compiler_params=_params("parallel", "arbitrary"),
        name=name,
    )(x, w, resid, nw.reshape(1, N))


def _transpose_exact(a):
    n = a.shape[1]
    eye = (lax.broadcasted_iota(jnp.int32, (n, n), 0)
           == lax.broadcasted_iota(jnp.int32, (n, n), 1)).astype(BF16)
    out = None
    r = a
    for _ in range(3):
        p = r.astype(BF16)
        r = r - p.astype(F32)
        t = lax.dot_general(eye, p, _NT, preferred_element_type=F32)
        out = t if out is None else out + t
    return out


def _ssd_kernel(z_ref, xr_ref, bcr_ref, dtr_ref, conv0_ref, h0_ref, cwx_ref, cwbc_ref,
                cbx_ref, cbbc_ref, dtb_ref, alog_ref, dsk_ref, nw_ref,
                y_ref, hout_ref, ext_ref, h_ref, yscr_ref, *, L, D, G, P):
    c = pl.program_id(1)
    nc = pl.num_programs(1)
    N = SSD_STATE
    npairs = D // LANES
    pairs_per_group = npairs // G
    heads_per_pair = LANES // P

    @pl.when(c == 0)
    def _():
        ext_ref[0:SUBLANES, :] = conv0_ref[...]
        h_ref[...] = h0_ref[...]

    ext_ref[SUBLANES:SUBLANES + L, 0:D] = xr_ref[...]
    ext_ref[SUBLANES:SUBLANES + L, D:2 * D] = bcr_ref[...]

    def conv(lo, w_ref, b_ref):
        acc = None
        for k in range(SSD_CONV):
            r0 = SUBLANES - (SSD_CONV - 1) + k
            term = ext_ref[r0:r0 + L, lo:lo + D] * w_ref[k:k + 1, :]
            acc = term if acc is None else acc + term
        return _silu(acc + b_ref[...])

    xs = conv(0, cwx_ref, cbx_ref)
    bc = conv(D, cwbc_ref, cbbc_ref)
    ext_ref[0:SUBLANES, :] = ext_ref[L:L + SUBLANES, :]

    dt = _softplus(dtr_ref[...] + dtb_ref[...])
    a = dt * (-jnp.exp(alog_ref[...]))
    row = lax.broadcasted_iota(jnp.int32, (L, LANES), 0)
    acs = a
    s = 1
    while s < L:
        acs = acs + jnp.where(row >= s, _roll_rows(acs, s), 0.0)
        s *= 2
    acs_t = _transpose_exact(acs)
    last = acs[L - 1:L, :]
    e_in = jnp.exp(acs)
    e_tail = jnp.exp(last - acs)
    e_all = jnp.exp(last)

    lane = lax.broadcasted_iota(jnp.int32, (L, LANES), 1)
    tri = (lax.broadcasted_iota(jnp.int32, (L, L), 0)
           >= lax.broadcasted_iota(jnp.int32, (L, L), 1))
    srow = lax.broadcasted_iota(jnp.int32, (LANES, LANES), 0)

    def per_head(arr, j, rows=lane):
        out = None
        for hh in reversed(range(heads_per_pair)):
            h = j * heads_per_pair + hh
            col = arr[:, h:h + 1]
            out = col if out is None else jnp.where(rows < (hh + 1) * P, col, out)
        return out

    for g in range(G):
        bg = bc[:, g * N:(g + 1) * N].astype(BF16)
        cg = bc[:, (G + g) * N:(G + g + 1) * N].astype(BF16)
        cb = lax.dot_general(cg, bg, _NT, preferred_element_type=F32)
        for jj in range(pairs_per_group):
            j = g * pairs_per_group + jj
            sl = slice(j * LANES, (j + 1) * LANES)
            dx = per_head(dt, j) * xs[:, sl]
            dxb = dx.astype(BF16)
            ydiag = None
            for hh in reversed(range(heads_per_pair)):
                h = j * heads_per_pair + hh
                seg = jnp.where(tri, acs[:, h:h + 1] - acs_t[h:h + 1, :], -jnp.inf)
                m = (cb * jnp.exp(seg)).astype(BF16)
                yh = jnp.dot(m, dxb, preferred_element_type=F32)
                ydiag = yh if ydiag is None else jnp.where(lane < (hh + 1) * P, yh, ydiag)
            hp = h_ref[j]
            ystate = lax.dot_general(cg, hp.astype(BF16), _NT, preferred_element_type=F32)
            yscr_ref[:, sl] = ydiag + ystate * per_head(e_in, j)
            dxw = (dx * per_head(e_tail, j)).astype(BF16)
            upd = lax.dot_general(dxw, bg, _TN, preferred_element_type=F32)
            scale = None
            for hh in reversed(range(heads_per_pair)):
                h = j * heads_per_pair + hh
                col = e_all[:, h:h + 1]
                scale = col if scale is None else jnp.where(srow < (hh + 1) * P, col, scale)
            h_ref[j] = hp * scale + upd

    y = (yscr_ref[...] + dsk_ref[...] * xs) * _silu(z_ref[...])
    gw = D // G
    for g in range(G):
        sl = slice(g * gw, (g + 1) * gw)
        seg = y[:, sl]
        ms = jnp.mean(seg * seg, axis=-1, keepdims=True)
        y_ref[:, sl] = (seg * lax.rsqrt(ms + EPS) * nw_ref[:, sl]).astype(y_ref.dtype)

    @pl.when(c == nc - 1)
    def _():
        hout_ref[...] = h_ref[...]


def _ssd(proj, dtr, conv0, h0, cw, cb, dtb, alog, dsk, nw, *, row0, nseq, T, L, D, G, P, name):
    nc = T // L
    blk0 = row0 // L
    npairs = D // LANES

    def rows(col):
        return pl.BlockSpec((L, D), lambda b, c: (blk0 + b * nc + c, col))

    def whole(shape):
        return pl.BlockSpec(shape, lambda b, c: (0,) * len(shape))

    return pl.pallas_call(
        functools.partial(_ssd_kernel, L=L, D=D, G=G, P=P),
        grid=(nseq, nc),
        in_specs=[
            rows(0), rows(1), rows(2),
            pl.BlockSpec((L, LANES), lambda b, c: (blk0 + b * nc + c, 0)),
            pl.BlockSpec((None, SUBLANES, 2 * D), lambda b, c: (b, 0, 0)),
            pl.BlockSpec((None, npairs, LANES, SSD_STATE), lambda b, c: (b, 0, 0, 0)),
            whole((SSD_CONV, D)), whole((SSD_CONV, D)), whole((1, D)), whole((1, D)),
            whole((1, LANES)), whole((1, LANES)), whole((1, D)), whole((1, D)),
        ],
        out_specs=[
            pl.BlockSpec((L, D), lambda b, c: (b * nc + c, 0)),
            pl.BlockSpec((None, npairs, LANES, SSD_STATE), lambda b, c: (b, 0, 0, 0)),
        ],
        out_shape=[
            jax.ShapeDtypeStruct((nseq * T, D), BF16),
            jax.ShapeDtypeStruct((nseq, npairs, LANES, SSD_STATE), F32),
        ],
        scratch_shapes=[
            pltpu.VMEM((SUBLANES + L, 2 * D), F32),
            pltpu.VMEM((npairs, LANES, SSD_STATE), F32),
            pltpu.VMEM((L, D), F32),
        ],
        compiler_params=_params("parallel", "arbitrary"),
        name=name,
    )(proj, proj, proj, dtr, conv0, h0, cw[:, :D], cw[:, D:], cb[:, :D], cb[:, D:],
      dtb, alog, dsk, nw)


def _hgrn_kernel(q_ref, f_ref, i_ref, g_ref, lb_ref, s0_ref, nw_ref, o_ref, sout_ref, st_ref,
                 *, L, D):
    c = pl.program_id(1)
    nc = pl.num_programs(1)
    H = D // HG_KEY

    @pl.when(c == 0)
    def _():
        st_ref[...] = s0_ref[...]

    lb = lb_ref[...]
    fz = f_ref[...]
    la = jnp.log(lb)
    lg = jnp.log1p(-lb) - _softplus(-fz)
    logf = jnp.maximum(la, lg) + jnp.log1p(jnp.exp(-jnp.abs(la - lg)))
    q = q_ref[...]
    k = (1.0 - lb) * jax.nn.sigmoid(-fz)
    v = i_ref[...]

    row = lax.broadcasted_iota(jnp.int32, (L, D), 0)
    r8 = row & (SUBLANES - 1)
    p = logf
    t = logf
    for s in (1, 2, 4):
        p = p + jnp.where(r8 >= s, _roll_rows(p, s), 0.0)
        t = t + jnp.where(r8 >= s, _roll_rows(t, s), _roll_rows(t, (s - SUBLANES) % L))
    levels = []
    m = SUBLANES
    while m < L:
        second = (row & (2 * m - 1)) >= m
        ex = jnp.exp(jnp.where(second, p, t - p))
        ql = jnp.where(second, q * ex, 0.0).astype(BF16)
        kl = jnp.where(second, 0.0, k * ex).astype(BF16)
        levels.append((m, ql, kl))
        prev_t = _roll_rows(t, m)
        next_t = _roll_rows(t, L - m)
        p = p + jnp.where(second, prev_t, 0.0)
        t = t + jnp.where(second, prev_t, next_t)
        m *= 2
    q_in = (q * jnp.exp(p)).astype(BF16)
    k_tail = (k * jnp.exp(t - p)).astype(BF16)
    dec = jnp.exp(t[0:1, :])
    vb = v.astype(BF16)

    rcol = lax.broadcasted_iota(jnp.int32, (L, 1), 0) & (SUBLANES - 1)
    rl = lax.broadcasted_iota(jnp.int32, (L, L), 0)
    cl = lax.broadcasted_iota(jnp.int32, (L, L), 1)

    for h in range(H):
        sl = slice(h * HG_KEY, (h + 1) * HG_KEY)
        qh, kh, vh, lf = q[:, sl], k[:, sl], v[:, sl], logf[:, sl]
        o = jnp.sum(qh * kh, axis=-1, keepdims=True) * vh
        e = None
        for d in range(1, SUBLANES):
            sh = lf if d == 1 else _roll_rows(lf, d - 1)
            e = sh if e is None else e + sh
            w = qh * jnp.exp(e) * _roll_rows(kh, d)
            att = jnp.where(rcol >= d, jnp.sum(w, axis=-1, keepdims=True), 0.0)
            o = o + att * _roll_rows(vh, d)
        amat = None
        for (m, ql, kl) in levels:
            sc = lax.dot_general(ql[:, sl], kl[:, sl], _NT, preferred_element_type=F32)
            sh2 = (2 * m).bit_length() - 1
            sc = jnp.where((rl >> sh2) == (cl >> sh2), sc, 0.0)
            amat = sc if amat is None else amat + sc
        if amat is not None:
            o = o + jnp.dot(amat.astype(BF16), vb[:, sl], preferred_element_type=F32)
        st = st_ref[h]
        o = o + lax.dot_general(q_in[:, sl], st.astype(BF16), _NT, preferred_element_type=F32)
        st_ref[h] = st * dec[:, sl] + lax.dot_general(vb[:, sl], k_tail[:, sl], _TN,
                                                      preferred_element_type=F32)
        ms = jnp.mean(o * o, axis=-1, keepdims=True)
        o = o * lax.rsqrt(ms + EPS)
        o_ref[:, sl] = (o * nw_ref[:, sl] * _silu(g_ref[:, sl])).astype(o_ref.dtype)

    @pl.when(c == nc - 1)
    def _():
        sout_ref[...] = st_ref[...]


def _hgrn(proj, lb, s0t, nw, *, row0, nseq, T, L, D, name):
    nc = T // L
    blk0 = row0 // L
    H = D // HG_KEY

    def rows(col):
        return pl.BlockSpec((L, D), lambda b, c: (blk0 + b * nc + c, col))

    return pl.pallas_call(
        functools.partial(_hgrn_kernel, L=L, D=D),
        grid=(nseq, nc),
        in_specs=[
            rows(3), rows(4), rows(5), rows(6),
            pl.BlockSpec((1, D), lambda b, c: (0, 0)),
            pl.BlockSpec((None, H, HG_KEY, HG_KEY), lambda b, c: (b, 0, 0, 0)),
            pl.BlockSpec((1, D), lambda b, c: (0, 0)),
        ],
        out_specs=[
            pl.BlockSpec((L, D), lambda b, c: (b * nc + c, 0)),
            pl.BlockSpec((None, H, HG_KEY, HG_KEY), lambda b, c: (b, 0, 0, 0)),
        ],
        out_shape=[
            jax.ShapeDtypeStruct((nseq * T, D), BF16),
            jax.ShapeDtypeStruct((nseq, H, HG_KEY, HG_KEY), F32),
        ],
        scratch_shapes=[pltpu.VMEM((H, HG_KEY, HG_KEY), F32)],
        compiler_params=_params("parallel", "arbitrary"),
        name=name,
    )(proj, proj, proj, proj, lb, s0t, nw)


def _pool_kernel(x_ref, buf_ref, nw0_ref, pw_ref, ps_ref, nw1_ref, xo_ref, tail_ref, ext_ref,
                 *, L, D, pos0):
    c = pl.program_id(1)
    hist = POOL_BUF + 1

    @pl.when(c == 0)
    def _():
        ext_ref[0:hist, :] = buf_ref[...]

    x = x_ref[...]
    ms = jnp.mean(x * x, axis=-1, keepdims=True)
    hn = x * lax.rsqrt(ms + EPS) * nw0_ref[...]
    ext_ref[hist:hist + L, :] = hn

    pos = pos0 + c * L + lax.broadcasted_iota(jnp.int32, (L, 1), 0)
    gw = D // len(POOL_WINDOWS)
    parts = []
    ss = None
    for gi, w in enumerate(POOL_WINDOWS):
        sl = slice(gi * gw, (gi + 1) * gw)
        s = ext_ref[:, sl]
        span = 1
        while span < w:
            s = s + _roll_rows(s, span)
            span *= 2
        cnt = jnp.minimum(pos + 1, w).astype(F32)
        pooled = s[hist:hist + L] / cnt - hn[:, sl]
        mixed = jnp.dot(pooled.astype(BF16), pw_ref[gi], preferred_element_type=F32) * ps_ref[:, sl]
        parts.append(mixed)
        sq = jnp.sum(mixed * mixed, axis=-1, keepdims=True)
        ss = sq if ss is None else ss + sq
    rs = lax.rsqrt(ss / D + EPS)
    for gi in range(len(POOL_WINDOWS)):
        sl = slice(gi * gw, (gi + 1) * gw)
        xo_ref[:, sl] = x[:, sl] + parts[gi] * rs * nw1_ref[:, sl]

    new_hist = ext_ref[L:L + hist, :]
    tail_ref[...] = new_hist
    ext_ref[0:hist, :] = new_hist


def _pool(x, buf, nw0, pw, ps, nw1, *, row0, nseq, T, L, D, pos0, name):
    nc = T // L
    blk0 = row0 // L
    hist = POOL_BUF + 1
    ng = len(POOL_WINDOWS)
    gw = D // ng
    vec = pl.BlockSpec((1, D), lambda b, c: (0, 0))
    return pl.pallas_call(
        functools.partial(_pool_kernel, L=L, D=D, pos0=pos0),
        grid=(nseq, nc),
        in_specs=[
            pl.BlockSpec((L, D), lambda b, c: (blk0 + b * nc + c, 0)),
            pl.BlockSpec((None, hist, D), lambda b, c: (b, 0, 0)),
            vec,
            pl.BlockSpec((ng, gw, gw), lambda b, c: (0, 0, 0)),
            vec, vec,
        ],
        out_specs=[
            pl.BlockSpec((L, D), lambda b, c: (b * nc + c, 0)),
            pl.BlockSpec((None, hist, D), lambda b, c: (b, 0, 0)),
        ],
        out_shape=[
            jax.ShapeDtypeStruct((nseq * T, D), F32),
            jax.ShapeDtypeStruct((nseq, hist, D), F32),
        ],
        scratch_shapes=[pltpu.VMEM((hist + L, D), F32)],
        compiler_params=_params("parallel", "arbitrary"),
        name=name,
    )(x, buf, nw0, pw, ps, nw1)


def _pad_lanes(v):
    return jnp.pad(v.astype(F32), (0, LANES - v.shape[0])).reshape(1, LANES)


def kernel(x_prompt, x_sample, state_conv, state_ssd, state_hgrn, state_pool, norm_w, w_in,
           conv_w, conv_b, dt_bias, a_log, d_skip, ssd_norm_w, hg_norm_w, hg_lower_bounds,
           w_out, pool_w, pool_scale, w_ffn_up, w_ffn_down):
    B, T, D = x_prompt.shape
    Bs, Ts, _ = x_sample.shape
    depth = norm_w.shape[0]
    n_ab = w_in.shape[0]
    heads = dt_bias.shape[1]
    P = D // heads
    conv_dim = conv_w.shape[-1]
    G = (conv_dim - D) // (2 * SSD_STATE)
    assert conv_dim == 2 * D and LANES % P == 0 and heads <= LANES and D % (G * LANES) == 0
    assert Ts >= POOL_BUF + 1 and T >= POOL_BUF + 1
    Mp, Ms = B * T, Bs * Ts
    npairs = D // LANES
    H = D // HG_KEY
    hist = POOL_BUF + 1

    l_ssd = _tile(T, (128, 64, 32, 16))
    l_hg = _tile(T, (64, 32, 16))
    l_pool = _tile(T, (256, 128, 64, 32, 16))
    segs = (
        dict(row0=0, nseq=B, T=T),
        dict(row0=Mp, nseq=Bs, T=Ts),
    )

    x = jnp.concatenate([x_prompt.reshape(Mp, D), x_sample.reshape(Ms, D)], axis=0)

    lbs = jnp.cumsum(jax.nn.softmax(hg_lower_bounds.astype(F32), axis=0), axis=0)
    lbs = lbs - lbs[0]

    zero_states = dict(
        conv=jnp.zeros((B, SUBLANES, conv_dim), F32),
        ssd=jnp.zeros((B, npairs, LANES, SSD_STATE), F32),
        hg=jnp.zeros((B, H, HG_KEY, HG_KEY), F32),
        pool=jnp.zeros((B, hist, D), F32),
    )

    conv_out, ssd_out, hg_out, pool_out = ([], []), ([], []), ([], []), ([], [])
    for layer in range(depth):
        j = layer // 2
        nw = norm_w[layer].astype(F32)
        if layer % 2 == 0:
            wj = w_in[j]
            w_main = jnp.concatenate([wj[:, :3 * D], wj[:, 3 * D + heads:]], axis=1).astype(BF16)
            w_dt = jnp.pad(wj[:, 3 * D:3 * D + heads], ((0, 0), (0, LANES - heads))).astype(BF16)
            proj = _norm_mm(x, nw[0], w_main, name=f"in_proj_{layer}")
            dtr = _norm_mm(x, nw[0], w_dt, name=f"dt_proj_{layer}")
            dsk = jnp.repeat(d_skip[j].astype(F32), P).reshape(1, D)
            ys, os_ = [], []
            for si, seg in enumerate(segs):
                nseq, tt = seg["nseq"], seg["T"]
                lss = l_ssd if si == 0 else _tile(tt, (128, 64, 32, 16))
                lhg = l_hg if si == 0 else _tile(tt, (64, 32, 16))
                if si == 0:
                    conv0, h0, s0t = zero_states["conv"], zero_states["ssd"], zero_states["hg"]
                else:
                    conv0 = jnp.pad(state_conv[j].astype(F32),
                                    ((0, 0), (SUBLANES - (SSD_CONV - 1), 0), (0, 0)))
                    h0 = state_ssd[j].astype(F32).reshape(nseq, npairs, LANES, SSD_STATE)
                    s0t = jnp.swapaxes(state_hgrn[j].astype(F32), -1, -2)
                y, hfin = _ssd(proj, dtr, conv0, h0, conv_w[j].astype(F32),
                               conv_b[j].astype(F32).reshape(1, conv_dim), _pad_lanes(dt_bias[j]),
                               _pad_lanes(a_log[j]), dsk, ssd_norm_w[j].astype(F32).reshape(1, D),
                               L=lss, D=D, G=G, P=P, name=f"ssd_{layer}_{si}", **seg)
                o, sfin = _hgrn(proj, lbs[j].reshape(1, D), s0t,
                                hg_norm_w[j].astype(F32).reshape(1, D),
                                L=lhg, D=D, name=f"hgrn_{layer}_{si}", **seg)
                ys.append(y)
                os_.append(o)
                r0 = seg["row0"]
                raw = proj[r0:r0 + nseq * tt].reshape(nseq, tt, 7 * D)
                conv_out[si].append(raw[:, tt - (SSD_CONV - 1):, D:3 * D])
                ssd_out[si].append(hfin.reshape(nseq, heads, P, SSD_STATE))
                hg_out[si].append(jnp.swapaxes(sfin, -1, -2))
            mixed = jnp.concatenate(
                [jnp.concatenate([ys[0], os_[0]], axis=1), jnp.concatenate([ys[1], os_[1]], axis=1)],
                axis=0)
            x = _mm_res_norm(mixed, w_out[j].astype(BF16), x, nw[1], name=f"out_proj_{layer}")
        else:
            xs_new = []
            for si, seg in enumerate(segs):
                nseq, tt = seg["nseq"], seg["T"]
                lp = l_pool if si == 0 else _tile(tt, (256, 128, 64, 32, 16))
                if si == 0:
                    buf = zero_states["pool"]
                else:
                    buf = jnp.pad(state_pool[j].astype(F32), ((0, 0), (1, 0), (0, 0)))
                xn, tail = _pool(x, buf, nw[0].reshape(1, D), pool_w[j].astype(BF16),
                                 pool_scale[j].astype(F32).reshape(1, D), nw[1].reshape(1, D),
                                 L=lp, D=D, pos0=0 if si == 0 else PAST_LEN,
                                 name=f"pool_{layer}_{si}", **seg)
                xs_new.append(xn)
                pool_out[si].append(tail[:, 1:, :])
            x = jnp.concatenate(xs_new, axis=0)
        h1 = _norm_mm(x, nw[2], w_ffn_up[layer].astype(BF16), act="relu2", out_dtype=BF16,
                      name=f"ffn_up_{layer}")
        x = _mm_res_norm(h1, w_ffn_down[layer].astype(BF16), x, nw[3], name=f"ffn_down_{layer}")

    dt_out = x_prompt.dtype
    y_prompt = x[:Mp].reshape(B, T, D).astype(dt_out)
    y_sample = x[Mp:].reshape(Bs, Ts, D).astype(dt_out)

    def stack(lst):
        return jnp.stack(lst).astype(dt_out)

    return (y_prompt, y_sample,
            stack(conv_out[0]), stack(ssd_out[0]), stack(hg_out[0]), stack(pool_out[0]),
            stack(conv_out[1]), stack(ssd_out[1]), stack(hg_out[1]), stack(pool_out[1]))
```

```python
import functools

import jax
import jax.numpy as jnp
from jax import lax
from jax.experimental import pallas as pl
from jax.experimental.pallas import tpu as pltpu

F32 = jnp.float32
BF16 = jnp.bfloat16
EPS = 1e-6
PAST_LEN = 4096
POOL_WINDOWS = (2, 4, 8, 16)
POOL_BUF = max(POOL_WINDOWS) - 1
SSD_STATE = 128
SSD_CONV = 4
HG_KEY = 128
LOG2_E = 1.4426950408889634
LANES = 128
SUBLANES = 8
VMEM_LIMIT_BYTES = 56 * 1024 * 1024

_NT = (((1,), (1,)), ((), ()))
_TN = (((0,), (0,)), ((), ()))


def _tile(n, candidates):
    for c in candidates:
        if n % c == 0:
            return c
    raise ValueError(f"no tile for {n} in {candidates}")


def _params(*sem):
    return pltpu.CompilerParams(dimension_semantics=sem, vmem_limit_bytes=VMEM_LIMIT_BYTES)


def _softplus(x):
    return jnp.maximum(x, 0.0) + jnp.log1p(jnp.exp(-jnp.abs(x)))


def _silu(x):
    return x * jax.nn.sigmoid(x)


def _roll_rows(x, shift):
    return pltpu.roll(x, shift, 0)


def _roll_in_block(x, shift):
    rows, width = x.shape
    x3 = x.reshape(rows // SUBLANES, SUBLANES, width)
    return pltpu.roll(x3, shift, 1).reshape(rows, width)


def _norm_mm_kernel(*refs, act, extra):
    if extra:
        x_ref, nw_ref, w_ref, w2_ref, o_ref, o2_ref, hn_ref = refs
    else:
        x_ref, nw_ref, w_ref, o_ref, hn_ref = refs

    @pl.when(pl.program_id(1) == 0)
    def _():
        x = x_ref[...]
        ms = jnp.mean(x * x, axis=-1, keepdims=True)
        hn_ref[...] = (x * lax.rsqrt(ms + EPS) * nw_ref[...]).astype(BF16)
        if extra:
            o2_ref[...] = jnp.dot(hn_ref[...], w2_ref[...], preferred_element_type=F32)

    acc = jnp.dot(hn_ref[...], w_ref[...], preferred_element_type=F32)
    if act == "relu2":
        acc = jnp.square(jnp.maximum(acc, 0.0))
    o_ref[...] = acc.astype(o_ref.dtype)


def _norm_mm(x, nw, w, w2=None, *, act=None, out_dtype=F32, name):
    M, K = x.shape
    N = w.shape[1]
    tm = _tile(M, (512, 256, 128, 64, 32, 16))
    tn = _tile(N, (1024, 512, 256, 128))
    extra = w2 is not None
    in_specs = [
        pl.BlockSpec((tm, K), lambda i, j: (i, 0)),
        pl.BlockSpec((1, K), lambda i, j: (0, 0)),
        pl.BlockSpec((K, tn), lambda i, j: (0, j)),
    ]
    out_specs = [pl.BlockSpec((tm, tn), lambda i, j: (i, j))]
    out_shape = [jax.ShapeDtypeStruct((M, N), out_dtype)]
    args = [x, nw.reshape(1, K), w]
    if extra:
        in_specs.append(pl.BlockSpec((K, LANES), lambda i, j: (0, 0)))
        out_specs.append(pl.BlockSpec((tm, LANES), lambda i, j: (i, 0)))
        out_shape.append(jax.ShapeDtypeStruct((M, LANES), F32))
        args.append(w2)
    res = pl.pallas_call(
        functools.partial(_norm_mm_kernel, act=act, extra=extra),
        grid=(M // tm, N // tn),
        in_specs=in_specs,
        out_specs=out_specs,
        out_shape=out_shape,
        scratch_shapes=[pltpu.VMEM((tm, K), BF16)],
        compiler_params=_params("parallel", "arbitrary"),
        name=name,
    )(*args)
    return res if extra else res[0]


def _mm_res_norm_kernel(x_ref, w_ref, r_ref, nw_ref, o_ref, acc_ref, *, nj, tn):
    j = pl.program_id(1)
    acc_ref[j] = jnp.dot(x_ref[...], w_ref[...], preferred_element_type=F32)

    @pl.when(j == nj - 1)
    def _():
        ss = None
        for jj in range(nj):
            a = acc_ref[jj]
            s = jnp.sum(a * a, axis=-1, keepdims=True)
            ss = s if ss is None else ss + s
        rs = lax.rsqrt(ss / (nj * tn) + EPS)
        for jj in range(nj):
            sl = slice(jj * tn, (jj + 1) * tn)
            o_ref[:, sl] = r_ref[:, sl] + acc_ref[jj] * rs * nw_ref[:, sl]


def _mm_res_norm(x, w, resid, nw, *, name):
    M, K = x.shape
    N = w.shape[1]
    tm = _tile(M, (512, 256, 128, 64, 32, 16))
    tn = _tile(N, (256, 128)) if K > 4096 else _tile(N, (512, 256, 128))
    nj = N // tn
    return pl.pallas_call(
        functools.partial(_mm_res_norm_kernel, nj=nj, tn=tn),
        grid=(M // tm, nj),
        in_specs=[
            pl.BlockSpec((tm, K), lambda i, j: (i, 0)),
            pl.BlockSpec((K, tn), lambda i, j: (0, j)),
            pl.BlockSpec((tm, N), lambda i, j: (i, 0)),
            pl.BlockSpec((1, N), lambda i, j: (0, 0)),
        ],
        out_specs=pl.BlockSpec((tm, N), lambda i, j: (i, 0)),
        out_shape=jax.ShapeDtypeStruct((M, N), F32),
        scratch_shapes=[pltpu.VMEM((nj, tm, tn), F32)],
        compiler_params=_params("parallel", "arbitrary"),
        name=name,
    )(x, w, resid, nw.reshape(1, N))


def _transpose_exact(a):
    n = a.shape[1]
    eye = (lax.broadcasted_iota(jnp.int32, (n, n), 0)
           == lax.broadcasted_iota(jnp.int32, (n, n), 1)).astype(BF16)
    out = None
    r = a
    for _ in range(3):
        p = r.astype(BF16)
        r = r - p.astype(F32)
        t = lax.dot_general(eye, p, _NT, preferred_element_type=F32)
        out = t if out is None else out + t
    return out


def _ssd_kernel(*refs, L, D, G, P, aliased):
    (z_ref, xr_ref, bcr_ref, dtr_ref, conv0_ref, h0_ref, cwx_ref, cwbc_ref,
     cbx_ref, cbbc_ref, dtb_ref, alog_ref, dsk_ref, nw_ref) = refs[:14]
    y_ref, hout_ref, cout_ref, ext_ref, h_ref, yscr_ref = refs[14 + int(aliased):]
    c = pl.program_id(1)
    nc = pl.num_programs(1)
    N = SSD_STATE
    npairs = D // LANES
    pairs_per_group = npairs // G
    heads_per_pair = LANES // P

    @pl.when(c == 0)
    def _():
        ext_ref[0:SUBLANES, :] = conv0_ref[...]
        h_ref[...] = h0_ref[...]

    ext_ref[SUBLANES:SUBLANES + L, 0:D] = xr_ref[...]
    ext_ref[SUBLANES:SUBLANES + L, D:2 * D] = bcr_ref[...]

    def conv(lo, w_ref, b_ref):
        acc = None
        for k in range(SSD_CONV):
            r0 = SUBLANES - (SSD_CONV - 1) + k
            term = ext_ref[r0:r0 + L, lo:lo + D] * w_ref[k:k + 1, :]
            acc = term if acc is None else acc + term
        return _silu(acc + b_ref[...])

    xs = conv(0, cwx_ref, cbx_ref)
    bc = conv(D, cwbc_ref, cbbc_ref)
    last_rows = ext_ref[L:L + SUBLANES, :]
    ext_ref[0:SUBLANES, :] = last_rows

    dt = _softplus(dtr_ref[...] + dtb_ref[...])
    a = dt * (-jnp.exp(alog_ref[...]))
    row = lax.broadcasted_iota(jnp.int32, (L, LANES), 0)
    acs = a
    s = 1
    while s < L:
        acs = acs + jnp.where(row >= s, _roll_rows(acs, s), 0.0)
        s *= 2
    acs_t = _transpose_exact(acs)
    last = acs[L - 1:L, :]
    e_in = jnp.exp(acs)
    e_tail = jnp.exp(last - acs)
    e_all = jnp.exp(last)

    lane = lax.broadcasted_iota(jnp.int32, (L, LANES), 1)
    tri = (lax.broadcasted_iota(jnp.int32, (L, L), 0)
           >= lax.broadcasted_iota(jnp.int32, (L, L), 1))
    srow = lax.broadcasted_iota(jnp.int32, (LANES, LANES), 0)

    def per_head(arr, j, rows=lane):
        out = None
        for hh in reversed(range(heads_per_pair)):
            h = j * heads_per_pair + hh
            col = arr[:, h:h + 1]
            out = col if out is None else jnp.where(rows < (hh + 1) * P, col, out)
        return out

    for g in range(G):
        bg = bc[:, g * N:(g + 1) * N].astype(BF16)
        cg = bc[:, (G + g) * N:(G + g + 1) * N].astype(BF16)
        cb = lax.dot_general(cg, bg, _NT, preferred_element_type=F32)
        for jj in range(pairs_per_group):
            j = g * pairs_per_group + jj
            sl = slice(j * LANES, (j + 1) * LANES)
            dx = per_head(dt, j) * xs[:, sl]
            dxb = dx.astype(BF16)
            ydiag = None
            for hh in reversed(range(heads_per_pair)):
                h = j * heads_per_pair + hh
                seg = jnp.where(tri, acs[:, h:h + 1] - acs_t[h:h + 1, :], -jnp.inf)
                m = (cb * jnp.exp(seg)).astype(BF16)
                yh = jnp.dot(m, dxb, preferred_element_type=F32)
                ydiag = yh if ydiag is None else jnp.where(lane < (hh + 1) * P, yh, ydiag)
            hp = h_ref[j]
            ystate = lax.dot_general(cg, hp.astype(BF16), _NT, preferred_element_type=F32)
            yscr_ref[:, sl] = ydiag + ystate * per_head(e_in, j)
            dxw = (dx * per_head(e_tail, j)).astype(BF16)
            upd = lax.dot_general(dxw, bg, _TN, preferred_element_type=F32)
            scale = None
            for hh in reversed(range(heads_per_pair)):
                h = j * heads_per_pair + hh
                col = e_all[:, h:h + 1]
                scale = col if scale is None else jnp.where(srow < (hh + 1) * P, col, scale)
            h_ref[j] = hp * scale + upd

    y = (yscr_ref[...] + dsk_ref[...] * xs) * _silu(z_ref[...])
    gw = D // G
    for g in range(G):
        sl = slice(g * gw, (g + 1) * gw)
        seg = y[:, sl]
        ms = jnp.mean(seg * seg, axis=-1, keepdims=True)
        y_ref[:, sl] = (seg * lax.rsqrt(ms + EPS) * nw_ref[:, sl]).astype(y_ref.dtype)

    @pl.when(c == nc - 1)
    def _():
        hout_ref[...] = h_ref[...]
        cout_ref[...] = last_rows


def _ssd(proj, dtr, conv0, h0, cw, cb, dtb, alog, dsk, nw, mixed, *, row0, nseq, T, L, D, G, P,
         name):
    M = proj.shape[0]
    nc = T // L
    blk0 = row0 // L
    npairs = D // LANES
    aliased = mixed is not None

    def rows(col):
        return pl.BlockSpec((L, D), lambda b, c: (blk0 + b * nc + c, col))

    def whole(shape):
        return pl.BlockSpec(shape, lambda b, c: (0,) * len(shape))

    in_specs = [
        rows(0), rows(1), rows(2),
        pl.BlockSpec((L, LANES), lambda b, c: (blk0 + b * nc + c, 0)),
        pl.BlockSpec((None, SUBLANES, 2 * D), lambda b, c: (b, 0, 0)),
        pl.BlockSpec((None, npairs, LANES, SSD_STATE), lambda b, c: (b, 0, 0, 0)),
        whole((SSD_CONV, D)), whole((SSD_CONV, D)), whole((1, D)), whole((1, D)),
        whole((1, LANES)), whole((1, LANES)), whole((1, D)), whole((1, D)),
    ]
    args = [proj, proj, proj, dtr, conv0, h0, cw[:, :D], cw[:, D:], cb[:, :D], cb[:, D:],
            dtb, alog, dsk, nw]
    if aliased:
        in_specs.append(pl.BlockSpec(memory_space=pl.ANY))
        args.append(mixed)
    return pl.pallas_call(
        functools.partial(_ssd_kernel, L=L, D=D, G=G, P=P, aliased=aliased),
        grid=(nseq, nc),
        in_specs=in_specs,
        out_specs=[
            pl.BlockSpec((L, D), lambda b, c: (blk0 + b * nc + c, 0)),
            pl.BlockSpec((None, npairs, LANES, SSD_STATE), lambda b, c: (b, 0, 0, 0)),
            pl.BlockSpec((None, SUBLANES, 2 * D), lambda b, c: (b, 0, 0)),
        ],
        out_shape=[
            jax.ShapeDtypeStruct((M, 2 * D), BF16),
            jax.ShapeDtypeStruct((nseq, npairs, LANES, SSD_STATE), F32),
            jax.ShapeDtypeStruct((nseq, SUBLANES, 2 * D), F32),
        ],
        scratch_shapes=[
            pltpu.VMEM((SUBLANES + L, 2 * D), F32),
            pltpu.VMEM((npairs, LANES, SSD_STATE), F32),
            pltpu.VMEM((L, D), F32),
        ],
        input_output_aliases={len(args) - 1: 0} if aliased else {},
        compiler_params=_params("parallel", "arbitrary"),
        name=name,
    )(*args)


def _hgrn_kernel(q_ref, f_ref, i_ref, g_ref, lb_ref, s0_ref, nw_ref, mixed_ref, o_ref, sout_ref,
                 st_ref, *, L, D):
    del mixed_ref
    c = pl.program_id(1)
    nc = pl.num_programs(1)
    H = D // HG_KEY

    @pl.when(c == 0)
    def _():
        st_ref[...] = s0_ref[...]

    lb = lb_ref[...]
    fz = f_ref[...]
    la = jnp.log(lb)
    lg = jnp.log1p(-lb) - (jnp.maximum(-fz, 0.0) + jnp.log(1.0 + jnp.exp(-jnp.abs(fz))))
    logf = jnp.maximum(la, lg) + jnp.log(1.0 + jnp.exp(-jnp.abs(la - lg)))
    q = q_ref[...]
    k = (1.0 - lb) * jax.nn.sigmoid(-fz)
    v = i_ref[...]
    lf2 = logf * LOG2_E
    f = jnp.exp2(lf2)

    row = lax.broadcasted_iota(jnp.int32, (L, D), 0)
    r8 = row & (SUBLANES - 1)
    p = lf2
    t = lf2
    for s in (1, 2, 4):
        p = p + jnp.where(r8 >= s, _roll_in_block(p, s), 0.0)
        t = t + _roll_in_block(t, s)
    levels = []
    m = SUBLANES
    while m < L:
        second = (row & (2 * m - 1)) >= m
        ex = jnp.exp2(jnp.where(second, p, t - p))
        levels.append((m, (q * ex).astype(BF16), (k * ex).astype(BF16)))
        prev_t = _roll_rows(t, m)
        next_t = _roll_rows(t, L - m)
        p = p + jnp.where(second, prev_t, 0.0)
        t = t + jnp.where(second, prev_t, next_t)
        m *= 2
    q_in = (q * jnp.exp2(p)).astype(BF16)
    k_tail = (k * jnp.exp2(t - p)).astype(BF16)
    dec = jnp.exp2(t[0:1, :])
    vb = v.astype(BF16)

    rl = lax.broadcasted_iota(jnp.int32, (L, L), 0)
    cl = lax.broadcasted_iota(jnp.int32, (L, L), 1)
    lag_masks = [(cl == rl - d) & ((rl & (SUBLANES - 1)) >= d) for d in range(SUBLANES)]
    level_masks = [((rl >> ((2 * m).bit_length() - 1)) == (cl >> ((2 * m).bit_length() - 1)))
                   & ((rl & (2 * m - 1)) >= m) & ((cl & (2 * m - 1)) < m)
                   for (m, _, _) in levels]

    for h in range(H):
        sl = slice(h * HG_KEY, (h + 1) * HG_KEY)
        qh, kh, fh = q[:, sl], k[:, sl], f[:, sl]
        amat = jnp.where(lag_masks[0], jnp.sum(qh * kh, axis=-1, keepdims=True), 0.0)
        fd = None
        for d in range(1, SUBLANES):
            sh = fh if d == 1 else _roll_in_block(fh, d - 1)
            fd = sh if fd is None else fd * sh
            w = qh * fd * _roll_in_block(kh, d)
            amat = jnp.where(lag_masks[d], jnp.sum(w, axis=-1, keepdims=True), amat)
        for (m, ql, kl), mask in zip(levels, level_masks):
            sc = lax.dot_general(ql[:, sl], kl[:, sl], _NT, preferred_element_type=F32)
            amat = jnp.where(mask, sc, amat)
        o = jnp.dot(amat.astype(BF16), vb[:, sl], preferred_element_type=F32)
        st = st_ref[h]
        o = o + lax.dot_general(q_in[:, sl], st.astype(BF16), _NT, preferred_element_type=F32)
        st_ref[h] = st * dec[:, sl] + lax.dot_general(vb[:, sl], k_tail[:, sl], _TN,
                                                      preferred_element_type=F32)
        ms = jnp.mean(o * o, axis=-1, keepdims=True)
        o = o * lax.rsqrt(ms + EPS)
        o_ref[:, sl] = (o * nw_ref[:, sl] * _silu(g_ref[:, sl])).astype(o_ref.dtype)

    @pl.when(c == nc - 1)
    def _():
        sout_ref[...] = st_ref[...]


def _hgrn(proj, lb, s0t, nw, mixed, *, row0, nseq, T, L, D, name):
    nc = T // L
    blk0 = row0 // L
    H = D // HG_KEY

    def rows(col):
        return pl.BlockSpec((L, D), lambda b, c: (blk0 + b * nc + c, col))

    return pl.pallas_call(
        functools.partial(_hgrn_kernel, L=L, D=D),
        grid=(nseq, nc),
        in_specs=[
            rows(3), rows(4), rows(5), rows(6),
            pl.BlockSpec((1, D), lambda b, c: (0, 0)),
            pl.BlockSpec((None, H, HG_KEY, HG_KEY), lambda b, c: (b, 0, 0, 0)),
            pl.BlockSpec((1, D), lambda b, c: (0, 0)),
            pl.BlockSpec(memory_space=pl.ANY),
        ],
        out_specs=[
            rows(1),
            pl.BlockSpec((None, H, HG_KEY, HG_KEY), lambda b, c: (b, 0, 0, 0)),
        ],
        out_shape=[
            jax.ShapeDtypeStruct(mixed.shape, mixed.dtype),
            jax.ShapeDtypeStruct((nseq, H, HG_KEY, HG_KEY), F32),
        ],
        scratch_shapes=[pltpu.VMEM((H, HG_KEY, HG_KEY), F32)],
        input_output_aliases={7: 0},
        compiler_params=_params("parallel", "arbitrary"),
        name=name,
    )(proj, proj, proj, proj, lb, s0t, nw, mixed)


def _pool_kernel(x_ref, buf_ref, nw0_ref, pw_ref, ps_ref, nw1_ref, xo_ref, tail_ref, ext_ref,
                 *, L, D, pos0):
    c = pl.program_id(1)
    hist = POOL_BUF + 1

    @pl.when(c == 0)
    def _():
        ext_ref[0:hist, :] = buf_ref[...]

    x = x_ref[...]
    ms = jnp.mean(x * x, axis=-1, keepdims=True)
    hn = x * lax.rsqrt(ms + EPS) * nw0_ref[...]
    ext_ref[hist:hist + L, :] = hn

    pos = pos0 + c * L + lax.broadcasted_iota(jnp.int32, (L, 1), 0)
    gw = D // len(POOL_WINDOWS)
    parts = []
    ss = None
    for gi, w in enumerate(POOL_WINDOWS):
        sl = slice(gi * gw, (gi + 1) * gw)
        s = ext_ref[:, sl]
        span = 1
        while span < w:
            s = s + _roll_rows(s, span)
            span *= 2
        cnt = jnp.minimum(pos + 1, w).astype(F32)
        pooled = s[hist:hist + L] / cnt - hn[:, sl]
        mixed = jnp.dot(pooled.astype(BF16), pw_ref[gi], preferred_element_type=F32) * ps_ref[:, sl]
        parts.append(mixed)
        sq = jnp.sum(mixed * mixed, axis=-1, keepdims=True)
        ss = sq if ss is None else ss + sq
    rs = lax.rsqrt(ss / D + EPS)
    for gi in range(len(POOL_WINDOWS)):
        sl = slice(gi * gw, (gi + 1) * gw)
        xo_ref[:, sl] = x[:, sl] + parts[gi] * rs * nw1_ref[:, sl]

    new_hist = ext_ref[L:L + hist, :]
    tail_ref[...] = new_hist
    ext_ref[0:hist, :] = new_hist


def _pool(x, buf, nw0, pw, ps, nw1, *, row0, nseq, T, L, D, pos0, name):
    nc = T // L
    blk0 = row0 // L
    hist = POOL_BUF + 1
    ng = len(POOL_WINDOWS)
    gw = D // ng
    vec = pl.BlockSpec((1, D), lambda b, c: (0, 0))
    xrows = pl.BlockSpec((L, D), lambda b, c: (blk0 + b * nc + c, 0))
    return pl.pallas_call(
        functools.partial(_pool_kernel, L=L, D=D, pos0=pos0),
        grid=(nseq, nc),
        in_specs=[
            xrows,
            pl.BlockSpec((None, hist, D), lambda b, c: (b, 0, 0)),
            vec,
            pl.BlockSpec((ng, gw, gw), lambda b, c: (0, 0, 0)),
            vec, vec,
        ],
        out_specs=[
            xrows,
            pl.BlockSpec((None, hist, D), lambda b, c: (b, 0, 0)),
        ],
        out_shape=[
            jax.ShapeDtypeStruct(x.shape, F32),
            jax.ShapeDtypeStruct((nseq, hist, D), F32),
        ],
        scratch_shapes=[pltpu.VMEM((hist + L, D), F32)],
        input_output_aliases={0: 0},
        compiler_params=_params("parallel", "arbitrary"),
        name=name,
    )(x, buf, nw0, pw, ps, nw1)


def _pad_lanes(v):
    return jnp.pad(v.astype(F32), (0, LANES - v.shape[0])).reshape(1, LANES)


def kernel(x_prompt, x_sample, state_conv, state_ssd, state_hgrn, state_pool, norm_w, w_in,
           conv_w, conv_b, dt_bias, a_log, d_skip, ssd_norm_w, hg_norm_w, hg_lower_bounds,
           w_out, pool_w, pool_scale, w_ffn_up, w_ffn_down):
    B, T, D = x_prompt.shape
    Bs, Ts, _ = x_sample.shape
    depth = norm_w.shape[0]
    heads = dt_bias.shape[1]
    P = D // heads
    conv_dim = conv_w.shape[-1]
    G = (conv_dim - D) // (2 * SSD_STATE)
    assert conv_dim == 2 * D and LANES % P == 0 and heads <= LANES and D % (G * LANES) == 0
    assert Ts >= POOL_BUF + 1 and T >= POOL_BUF + 1
    Mp, Ms = B * T, Bs * Ts
    npairs = D // LANES
    H = D // HG_KEY
    hist = POOL_BUF + 1
    keep = SSD_CONV - 1

    segs = (
        dict(row0=0, nseq=B, T=T),
        dict(row0=Mp, nseq=Bs, T=Ts),
    )

    x = jnp.concatenate([x_prompt.reshape(Mp, D), x_sample.reshape(Ms, D)], axis=0)

    lbs = jnp.cumsum(jax.nn.softmax(hg_lower_bounds.astype(F32), axis=0), axis=0)
    lbs = lbs - lbs[0]

    conv_out, ssd_out, hg_out, pool_out = ([], []), ([], []), ([], []), ([], [])
    for layer in range(depth):
        j = layer // 2
        nw = norm_w[layer].astype(F32)
        if layer % 2 == 0:
            wj = w_in[j]
            w_main = jnp.concatenate([wj[:, :3 * D], wj[:, 3 * D + heads:]], axis=1).astype(BF16)
            w_dt = jnp.pad(wj[:, 3 * D:3 * D + heads], ((0, 0), (0, LANES - heads))).astype(BF16)
            proj, dtr = _norm_mm(x, nw[0], w_main, w_dt, name=f"in_proj_{layer}")
            dsk = jnp.repeat(d_skip[j].astype(F32), P).reshape(1, D)
            mixed = None
            for si, seg in enumerate(segs):
                nseq, tt = seg["nseq"], seg["T"]
                if si == 0:
                    conv0 = jnp.zeros((nseq, SUBLANES, conv_dim), F32)
                    h0 = jnp.zeros((nseq, npairs, LANES, SSD_STATE), F32)
                else:
                    conv0 = jnp.pad(state_conv[j].astype(F32),
                                    ((0, 0), (SUBLANES - keep, 0), (0, 0)))
                    h0 = state_ssd[j].astype(F32).reshape(nseq, npairs, LANES, SSD_STATE)
                mixed, hfin, craw = _ssd(
                    proj, dtr, conv0, h0, conv_w[j].astype(F32),
                    conv_b[j].astype(F32).reshape(1, conv_dim), _pad_lanes(dt_bias[j]),
                    _pad_lanes(a_log[j]), dsk, ssd_norm_w[j].astype(F32).reshape(1, D), mixed,
                    L=_tile(tt, (128, 64, 32, 16)), D=D, G=G, P=P, name=f"ssd_{layer}_{si}", **seg)
                conv_out[si].append(craw[:, SUBLANES - keep:, :])
                ssd_out[si].append(hfin.reshape(nseq, heads, P, SSD_STATE))
            for si, seg in enumerate(segs):
                nseq, tt = seg["nseq"], seg["T"]
                if si == 0:
                    s0t = jnp.zeros((nseq, H, HG_KEY, HG_KEY), F32)
                else:
                    s0t = jnp.swapaxes(state_hgrn[j].astype(F32), -1, -2)
                mixed, sfin = _hgrn(proj, lbs[j].reshape(1, D), s0t,
                                    hg_norm_w[j].astype(F32).reshape(1, D), mixed,
                                    L=_tile(tt, (64, 32, 16)), D=D, name=f"hgrn_{layer}_{si}", **seg)
                hg_out[si].append(jnp.swapaxes(sfin, -1, -2))
            x = _mm_res_norm(mixed, w_out[j].astype(BF16), x, nw[1], name=f"out_proj_{layer}")
        else:
            for si, seg in enumerate(segs):
                nseq, tt = seg["nseq"], seg["T"]
                if si == 0:
                    buf = jnp.zeros((nseq, hist, D), F32)
                else:
                    buf = jnp.pad(state_pool[j].astype(F32), ((0, 0), (1, 0), (0, 0)))
                x, tail = _pool(x, buf, nw[0].reshape(1, D), pool_w[j].astype(BF16),
                                pool_scale[j].astype(F32).reshape(1, D), nw[1].reshape(1, D),
                                L=_tile(tt, (256, 128, 64, 32, 16)), D=D,
                                pos0=0 if si == 0 else PAST_LEN, name=f"pool_{layer}_{si}", **seg)
                pool_out[si].append(tail[:, 1:, :])
        h1 = _norm_mm(x, nw[2], w_ffn_up[layer].astype(BF16), act="relu2", out_dtype=BF16,
                      name=f"ffn_up_{layer}")
        x = _mm_res_norm(h1, w_ffn_down[layer].astype(BF16), x, nw[3], name=f"ffn_down_{layer}")

    dt_out = x_prompt.dtype
    y_prompt = x[:Mp].reshape(B, T, D).astype(dt_out)
    y_sample = x[Mp:].reshape(Bs, Ts, D).astype(dt_out)

    def stack(lst):
        return jnp.stack(lst).astype(dt_out)

    return (y_prompt, y_sample,
            stack(conv_out[0]), stack(ssd_out[0]), stack(hg_out[0]), stack(pool_out[0]),
            stack(conv_out[1]), stack(ssd_out[1]), stack(hg_out[1]), stack(pool_out[1]))
```

```python
import functools

import jax
import jax.numpy as jnp
from jax import lax
from jax.experimental import pallas as pl
from jax.experimental.pallas import tpu as pltpu

F32 = jnp.float32
BF16 = jnp.bfloat16
EPS = 1e-6
PAST_LEN = 4096
POOL_WINDOWS = (2, 4, 8, 16)
POOL_BUF = max(POOL_WINDOWS) - 1
SSD_STATE = 128
SSD_CONV = 4
HG_KEY = 128
LOG2_E = 1.4426950408889634
LANES = 128
SUBLANES = 8
VMEM_LIMIT_BYTES = 56 * 1024 * 1024

_NT = (((1,), (1,)), ((), ()))
_TN = (((0,), (0,)), ((), ()))


def _tile(n, candidates):
    for c in candidates:
        if n % c == 0:
            return c
    raise ValueError(f"no tile for {n} in {candidates}")


def _params(*sem):
    return pltpu.CompilerParams(dimension_semantics=sem, vmem_limit_bytes=VMEM_LIMIT_BYTES)


def _softplus(x):
    return jnp.maximum(x, 0.0) + jnp.log1p(jnp.exp(-jnp.abs(x)))


def _silu(x):
    return x * jax.nn.sigmoid(x)


def _roll_rows(x, shift):
    return pltpu.roll(x, shift, 0)


def _roll_in_block(x, shift):
    rows, width = x.shape
    x3 = x.reshape(rows // SUBLANES, SUBLANES, width)
    return pltpu.roll(x3, shift, 1).reshape(rows, width)


ROW_TILES = (1280, 512, 256, 192, 128, 64, 32, 16)
ROW_CHUNK = 640
ACC_COLS = 512


def _row_chunks(tm):
    rc = ROW_CHUNK if tm % ROW_CHUNK == 0 else tm
    return [slice(r0, r0 + rc) for r0 in range(0, tm, rc)]


def _rms_rows(x, w):
    ms = jnp.mean(x * x, axis=-1, keepdims=True)
    return x * lax.rsqrt(ms + EPS) * w


def _accumulate(acc_ref, rows, lhs, w_ref):
    n = acc_ref.shape[1]
    step = min(ACC_COLS, n)
    for n0 in range(0, n, step):
        cols = slice(n0, n0 + step)
        acc_ref[rows, cols] += jnp.dot(lhs, w_ref[:, cols], preferred_element_type=F32)


def _norm_mm_kernel(x_ref, nw_ref, w_ref, w2_ref, o_ref, o2_ref, hn_ref):
    chunks = _row_chunks(x_ref.shape[0])

    @pl.when(pl.program_id(1) == 0)
    def _():
        for rows in chunks:
            hn_ref[rows, :] = _rms_rows(x_ref[rows, :], nw_ref[...]).astype(BF16)
            o2_ref[rows, :] = jnp.dot(hn_ref[rows, :], w2_ref[...], preferred_element_type=F32)

    for rows in chunks:
        o_ref[rows, :] = jnp.dot(hn_ref[rows, :], w_ref[...],
                                 preferred_element_type=F32).astype(o_ref.dtype)


def _norm_mm(x, nw, w, w2, *, name):
    M, K = x.shape
    N = w.shape[1]
    tm = _tile(M, ROW_TILES)
    tn = _tile(N, (1024, 512, 256, 128))
    return pl.pallas_call(
        _norm_mm_kernel,
        grid=(M // tm, N // tn),
        in_specs=[
            pl.BlockSpec((tm, K), lambda i, j: (i, 0)),
            pl.BlockSpec((1, K), lambda i, j: (0, 0)),
            pl.BlockSpec((K, tn), lambda i, j: (0, j)),
            pl.BlockSpec((K, LANES), lambda i, j: (0, 0)),
        ],
        out_specs=[
            pl.BlockSpec((tm, tn), lambda i, j: (i, j)),
            pl.BlockSpec((tm, LANES), lambda i, j: (i, 0)),
        ],
        out_shape=[
            jax.ShapeDtypeStruct((M, N), F32),
            jax.ShapeDtypeStruct((M, LANES), F32),
        ],
        scratch_shapes=[pltpu.VMEM((tm, K), BF16)],
        compiler_params=_params("parallel", "arbitrary"),
        name=name,
    )(x, nw.reshape(1, K), w, w2)


def _mm_res_norm_kernel(x_ref, w_ref, r_ref, nw_ref, o_ref, *, nk):
    k = pl.program_id(1)
    chunks = _row_chunks(x_ref.shape[0])

    @pl.when(k == 0)
    def _():
        o_ref[...] = jnp.zeros_like(o_ref)

    for rows in chunks:
        _accumulate(o_ref, rows, x_ref[rows, :], w_ref)

    @pl.when(k == nk - 1)
    def _():
        for rows in chunks:
            o_ref[rows, :] = r_ref[rows, :] + _rms_rows(o_ref[rows, :], nw_ref[...])


def _mm_res_norm(x, w, resid, nw, *, name):
    M, K = x.shape
    N = w.shape[1]
    tm = _tile(M, ROW_TILES)
    tk = _tile(K, (1024, 512, 256, 128))
    nk = K // tk
    once = pl.Buffered(1)
    return pl.pallas_call(
        functools.partial(_mm_res_norm_kernel, nk=nk),
        grid=(M // tm, nk),
        in_specs=[
            pl.BlockSpec((tm, tk), lambda i, k: (i, k)),
            pl.BlockSpec((tk, N), lambda i, k: (k, 0)),
            pl.BlockSpec((tm, N), lambda i, k: (i, 0), pipeline_mode=once),
            pl.BlockSpec((1, N), lambda i, k: (0, 0)),
        ],
        out_specs=pl.BlockSpec((tm, N), lambda i, k: (i, 0), pipeline_mode=once),
        out_shape=jax.ShapeDtypeStruct((M, N), F32),
        compiler_params=_params("parallel", "arbitrary"),
        name=name,
    )(x, w, resid, nw.reshape(1, N))


def _ffn_kernel(x_ref, nw_in_ref, wu_ref, wd_ref, nw_out_ref, o_ref, hn_ref, *, nf):
    f = pl.program_id(1)
    chunks = _row_chunks(x_ref.shape[0])

    @pl.when(f == 0)
    def _():
        o_ref[...] = jnp.zeros_like(o_ref)
        for rows in chunks:
            hn_ref[rows, :] = _rms_rows(x_ref[rows, :], nw_in_ref[...]).astype(BF16)

    for rows in chunks:
        h = jnp.dot(hn_ref[rows, :], wu_ref[...], preferred_element_type=F32)
        h = jnp.square(jnp.maximum(h, 0.0)).astype(BF16)
        _accumulate(o_ref, rows, h, wd_ref)

    @pl.when(f == nf - 1)
    def _():
        for rows in chunks:
            o_ref[rows, :] = x_ref[rows, :] + _rms_rows(o_ref[rows, :], nw_out_ref[...])


def _ffn(x, nw_in, wu, wd, nw_out, *, name):
    M, D = x.shape
    F = wu.shape[1]
    tm = _tile(M, ROW_TILES)
    tf = _tile(F, (512, 256, 128))
    nf = F // tf
    vec = pl.BlockSpec((1, D), lambda i, f: (0, 0))
    return pl.pallas_call(
        functools.partial(_ffn_kernel, nf=nf),
        grid=(M // tm, nf),
        in_specs=[
            pl.BlockSpec((tm, D), lambda i, f: (i, 0)),
            vec,
            pl.BlockSpec((D, tf), lambda i, f: (0, f)),
            pl.BlockSpec((tf, D), lambda i, f: (f, 0)),
            vec,
        ],
        out_specs=pl.BlockSpec((tm, D), lambda i, f: (i, 0), pipeline_mode=pl.Buffered(1)),
        out_shape=jax.ShapeDtypeStruct((M, D), F32),
        scratch_shapes=[pltpu.VMEM((tm, D), BF16)],
        compiler_params=_params("parallel", "arbitrary"),
        name=name,
    )(x, nw_in.reshape(1, D), wu, wd, nw_out.reshape(1, D))


def _transpose_exact(a):
    n = a.shape[1]
    eye = (lax.broadcasted_iota(jnp.int32, (n, n), 0)
           == lax.broadcasted_iota(jnp.int32, (n, n), 1)).astype(BF16)
    out = None
    r = a
    for _ in range(3):
        p = r.astype(BF16)
        r = r - p.astype(F32)
        t = lax.dot_general(eye, p, _NT, preferred_element_type=F32)
        out = t if out is None else out + t
    return out


def _ssd_kernel(*refs, L, D, G, P, aliased):
    (z_ref, xr_ref, bcr_ref, dtr_ref, conv0_ref, h0_ref, cwx_ref, cwbc_ref,
     cbx_ref, cbbc_ref, dtb_ref, alog_ref, dsk_ref, nw_ref) = refs[:14]
    y_ref, hout_ref, cout_ref, ext_ref, h_ref, yscr_ref = refs[14 + int(aliased):]
    c = pl.program_id(1)
    nc = pl.num_programs(1)
    N = SSD_STATE
    npairs = D // LANES
    pairs_per_group = npairs // G
    heads_per_pair = LANES // P

    @pl.when(c == 0)
    def _():
        ext_ref[0:SUBLANES, :] = conv0_ref[...]
        h_ref[...] = h0_ref[...]

    ext_ref[SUBLANES:SUBLANES + L, 0:D] = xr_ref[...]
    ext_ref[SUBLANES:SUBLANES + L, D:2 * D] = bcr_ref[...]

    def conv(lo, w_ref, b_ref):
        acc = None
        for k in range(SSD_CONV):
            r0 = SUBLANES - (SSD_CONV - 1) + k
            term = ext_ref[r0:r0 + L, lo:lo + D] * w_ref[k:k + 1, :]
            acc = term if acc is None else acc + term
        return _silu(acc + b_ref[...])

    xs = conv(0, cwx_ref, cbx_ref)
    bc = conv(D, cwbc_ref, cbbc_ref)
    last_rows = ext_ref[L:L + SUBLANES, :]
    ext_ref[0:SUBLANES, :] = last_rows

    dt = _softplus(dtr_ref[...] + dtb_ref[...])
    a = dt * (-jnp.exp(alog_ref[...]))
    row = lax.broadcasted_iota(jnp.int32, (L, LANES), 0)
    acs = a
    s = 1
    while s < L:
        acs = acs + jnp.where(row >= s, _roll_rows(acs, s), 0.0)
        s *= 2
    acs_t = _transpose_exact(acs)
    last = acs[L - 1:L, :]
    e_in = jnp.exp(acs)
    e_tail = jnp.exp(last - acs)
    e_all = jnp.exp(last)

    lane = lax.broadcasted_iota(jnp.int32, (L, LANES), 1)
    tri = (lax.broadcasted_iota(jnp.int32, (L, L), 0)
           >= lax.broadcasted_iota(jnp.int32, (L, L), 1))
    srow = lax.broadcasted_iota(jnp.int32, (LANES, LANES), 0)

    def per_head(arr, j, rows=lane):
        out = None
        for hh in reversed(range(heads_per_pair)):
            h = j * heads_per_pair + hh
            col = arr[:, h:h + 1]
            out = col if out is None else jnp.where(rows < (hh + 1) * P, col, out)
        return out

    for g in range(G):
        bg = bc[:, g * N:(g + 1) * N].astype(BF16)
        cg = bc[:, (G + g) * N:(G + g + 1) * N].astype(BF16)
        cb = lax.dot_general(cg, bg, _NT, preferred_element_type=F32)
        for jj in range(pairs_per_group):
            j = g * pairs_per_group + jj
            sl = slice(j * LANES, (j + 1) * LANES)
            dx = per_head(dt, j) * xs[:, sl]
            dxb = dx.astype(BF16)
            ydiag = None
            for hh in reversed(range(heads_per_pair)):
                h = j * heads_per_pair + hh
                seg = jnp.where(tri, acs[:, h:h + 1] - acs_t[h:h + 1, :], -jnp.inf)
                m = (cb * jnp.exp(seg)).astype(BF16)
                yh = jnp.dot(m, dxb, preferred_element_type=F32)
                ydiag = yh if ydiag is None else jnp.where(lane < (hh + 1) * P, yh, ydiag)
            hp = h_ref[j]
            ystate = lax.dot_general(cg, hp.astype(BF16), _NT, preferred_element_type=F32)
            yscr_ref[:, sl] = ydiag + ystate * per_head(e_in, j)
            dxw = (dx * per_head(e_tail, j)).astype(BF16)
            upd = lax.dot_general(dxw, bg, _TN, preferred_element_type=F32)
            scale = None
            for hh in reversed(range(heads_per_pair)):
                h = j * heads_per_pair + hh
                col = e_all[:, h:h + 1]
                scale = col if scale is None else jnp.where(srow < (hh + 1) * P, col, scale)
            h_ref[j] = hp * scale + upd

    y = (yscr_ref[...] + dsk_ref[...] * xs) * _silu(z_ref[...])
    gw = D // G
    for g in range(G):
        sl = slice(g * gw, (g + 1) * gw)
        seg = y[:, sl]
        ms = jnp.mean(seg * seg, axis=-1, keepdims=True)
        y_ref[:, sl] = (seg * lax.rsqrt(ms + EPS) * nw_ref[:, sl]).astype(y_ref.dtype)

    @pl.when(c == nc - 1)
    def _():
        hout_ref[...] = h_ref[...]
        cout_ref[...] = last_rows


def _ssd(proj, dtr, conv0, h0, cw, cb, dtb, alog, dsk, nw, mixed, *, row0, nseq, T, L, D, G, P,
         name):
    M = proj.shape[0]
    nc = T // L
    blk0 = row0 // L
    npairs = D // LANES
    aliased = mixed is not None

    def rows(col):
        return pl.BlockSpec((L, D), lambda b, c: (blk0 + b * nc + c, col))

    def whole(shape):
        return pl.BlockSpec(shape, lambda b, c: (0,) * len(shape))

    in_specs = [
        rows(0), rows(1), rows(2),
        pl.BlockSpec((L, LANES), lambda b, c: (blk0 + b * nc + c, 0)),
        pl.BlockSpec((None, SUBLANES, 2 * D), lambda b, c: (b, 0, 0)),
        pl.BlockSpec((None, npairs, LANES, SSD_STATE), lambda b, c: (b, 0, 0, 0)),
        whole((SSD_CONV, D)), whole((SSD_CONV, D)), whole((1, D)), whole((1, D)),
        whole((1, LANES)), whole((1, LANES)), whole((1, D)), whole((1, D)),
    ]
    args = [proj, proj, proj, dtr, conv0, h0, cw[:, :D], cw[:, D:], cb[:, :D], cb[:, D:],
            dtb, alog, dsk, nw]
    if aliased:
        in_specs.append(pl.BlockSpec(memory_space=pl.ANY))
        args.append(mixed)
    return pl.pallas_call(
        functools.partial(_ssd_kernel, L=L, D=D, G=G, P=P, aliased=aliased),
        grid=(nseq, nc),
        in_specs=in_specs,
        out_specs=[
            pl.BlockSpec((L, D), lambda b, c: (blk0 + b * nc + c, 0)),
            pl.BlockSpec((None, npairs, LANES, SSD_STATE), lambda b, c: (b, 0, 0, 0)),
            pl.BlockSpec((None, SUBLANES, 2 * D), lambda b, c: (b, 0, 0)),
        ],
        out_shape=[
            jax.ShapeDtypeStruct((M, 2 * D), BF16),
            jax.ShapeDtypeStruct((nseq, npairs, LANES, SSD_STATE), F32),
            jax.ShapeDtypeStruct((nseq, SUBLANES, 2 * D), F32),
        ],
        scratch_shapes=[
            pltpu.VMEM((SUBLANES + L, 2 * D), F32),
            pltpu.VMEM((npairs, LANES, SSD_STATE), F32),
            pltpu.VMEM((L, D), F32),
        ],
        input_output_aliases={len(args) - 1: 0} if aliased else {},
        compiler_params=_params("parallel", "arbitrary"),
        name=name,
    )(*args)


def _hgrn_kernel(q_ref, f_ref, i_ref, g_ref, lb_ref, s0_ref, nw_ref, mixed_ref, o_ref, sout_ref,
                 st_ref, *, L, D):
    del mixed_ref
    c = pl.program_id(1)
    nc = pl.num_programs(1)
    H = D // HG_KEY

    @pl.when(c == 0)
    def _():
        st_ref[...] = s0_ref[...]

    lb = lb_ref[...]
    fz = f_ref[...]
    la = jnp.log(lb)
    lg = jnp.log1p(-lb) - (jnp.maximum(-fz, 0.0) + jnp.log(1.0 + jnp.exp(-jnp.abs(fz))))
    logf = jnp.maximum(la, lg) + jnp.log(1.0 + jnp.exp(-jnp.abs(la - lg)))
    q = q_ref[...]
    k = (1.0 - lb) * jax.nn.sigmoid(-fz)
    v = i_ref[...]
    lf2 = logf * LOG2_E
    f = jnp.exp2(lf2)

    row = lax.broadcasted_iota(jnp.int32, (L, D), 0)
    r8 = row & (SUBLANES - 1)
    p = lf2
    t = lf2
    for s in (1, 2, 4):
        p = p + jnp.where(r8 >= s, _roll_in_block(p, s), 0.0)
        t = t + _roll_in_block(t, s)
    levels = []
    m = SUBLANES
    while m < L:
        second = (row & (2 * m - 1)) >= m
        ex = jnp.exp2(jnp.where(second, p, t - p))
        levels.append((m, (q * ex).astype(BF16), (k * ex).astype(BF16)))
        prev_t = _roll_rows(t, m)
        next_t = _roll_rows(t, L - m)
        p = p + jnp.where(second, prev_t, 0.0)
        t = t + jnp.where(second, prev_t, next_t)
        m *= 2
    q_in = (q * jnp.exp2(p)).astype(BF16)
    k_tail = (k * jnp.exp2(t - p)).astype(BF16)
    dec = jnp.exp2(t[0:1, :])
    vb = v.astype(BF16)

    rl = lax.broadcasted_iota(jnp.int32, (L, L), 0)
    cl = lax.broadcasted_iota(jnp.int32, (L, L), 1)
    lag_masks = [(cl == rl - d) & ((rl & (SUBLANES - 1)) >= d) for d in range(SUBLANES)]
    level_masks = [((rl >> ((2 * m).bit_length() - 1)) == (cl >> ((2 * m).bit_length() - 1)))
                   & ((rl & (2 * m - 1)) >= m) & ((cl & (2 * m - 1)) < m)
                   for (m, _, _) in levels]

    for h in range(H):
        sl = slice(h * HG_KEY, (h + 1) * HG_KEY)
        qh, kh, fh = q[:, sl], k[:, sl], f[:, sl]
        amat = jnp.where(lag_masks[0], jnp.sum(qh * kh, axis=-1, keepdims=True), 0.0)
        fd = None
        for d in range(1, SUBLANES):
            sh = fh if d == 1 else _roll_in_block(fh, d - 1)
            fd = sh if fd is None else fd * sh
            w = qh * fd * _roll_in_block(kh, d)
            amat = jnp.where(lag_masks[d], jnp.sum(w, axis=-1, keepdims=True), amat)
        for (m, ql, kl), mask in zip(levels, level_masks):
            sc = lax.dot_general(ql[:, sl], kl[:, sl], _NT, preferred_element_type=F32)
            amat = jnp.where(mask, sc, amat)
        o = jnp.dot(amat.astype(BF16), vb[:, sl], preferred_element_type=F32)
        st = st_ref[h]
        o = o + lax.dot_general(q_in[:, sl], st.astype(BF16), _NT, preferred_element_type=F32)
        st_ref[h] = st * dec[:, sl] + lax.dot_general(vb[:, sl], k_tail[:, sl], _TN,
                                                      preferred_element_type=F32)
        ms = jnp.mean(o * o, axis=-1, keepdims=True)
        o = o * lax.rsqrt(ms + EPS)
        o_ref[:, sl] = (o * nw_ref[:, sl] * _silu(g_ref[:, sl])).astype(o_ref.dtype)

    @pl.when(c == nc - 1)
    def _():
        sout_ref[...] = st_ref[...]


def _hgrn(proj, lb, s0t, nw, mixed, *, row0, nseq, T, L, D, name):
    nc = T // L
    blk0 = row0 // L
    H = D // HG_KEY

    def rows(col):
        return pl.BlockSpec((L, D), lambda b, c: (blk0 + b * nc + c, col))

    return pl.pallas_call(
        functools.partial(_hgrn_kernel, L=L, D=D),
        grid=(nseq, nc),
        in_specs=[
            rows(3), rows(4), rows(5), rows(6),
            pl.BlockSpec((1, D), lambda b, c: (0, 0)),
            pl.BlockSpec((None, H, HG_KEY, HG_KEY), lambda b, c: (b, 0, 0, 0)),
            pl.BlockSpec((1, D), lambda b, c: (0, 0)),
            pl.BlockSpec(memory_space=pl.ANY),
        ],
        out_specs=[
            rows(1),
            pl.BlockSpec((None, H, HG_KEY, HG_KEY), lambda b, c: (b, 0, 0, 0)),
        ],
        out_shape=[
            jax.ShapeDtypeStruct(mixed.shape, mixed.dtype),
            jax.ShapeDtypeStruct((nseq, H, HG_KEY, HG_KEY), F32),
        ],
        scratch_shapes=[pltpu.VMEM((H, HG_KEY, HG_KEY), F32)],
        input_output_aliases={7: 0},
        compiler_params=_params("parallel", "arbitrary"),
        name=name,
    )(proj, proj, proj, proj, lb, s0t, nw, mixed)


def _pool_kernel(x_ref, buf_ref, nw0_ref, pw_ref, ps_ref, nw1_ref, xo_ref, tail_ref, ext_ref,
                 *, L, D, pos0):
    c = pl.program_id(1)
    hist = POOL_BUF + 1

    @pl.when(c == 0)
    def _():
        ext_ref[0:hist, :] = buf_ref[...]

    x = x_ref[...]
    ms = jnp.mean(x * x, axis=-1, keepdims=True)
    hn = x * lax.rsqrt(ms + EPS) * nw0_ref[...]
    ext_ref[hist:hist + L, :] = hn

    pos = pos0 + c * L + lax.broadcasted_iota(jnp.int32, (L, 1), 0)
    gw = D // len(POOL_WINDOWS)
    parts = []
    ss = None
    for gi, w in enumerate(POOL_WINDOWS):
        sl = slice(gi * gw, (gi + 1) * gw)
        s = ext_ref[:, sl]
        span = 1
        while span < w:
            s = s + _roll_rows(s, span)
            span *= 2
        cnt = jnp.minimum(pos + 1, w).astype(F32)
        pooled = s[hist:hist + L] / cnt - hn[:, sl]
        mixed = jnp.dot(pooled.astype(BF16), pw_ref[gi], preferred_element_type=F32) * ps_ref[:, sl]
        parts.append(mixed)
        sq = jnp.sum(mixed * mixed, axis=-1, keepdims=True)
        ss = sq if ss is None else ss + sq
    rs = lax.rsqrt(ss / D + EPS)
    for gi in range(len(POOL_WINDOWS)):
        sl = slice(gi * gw, (gi + 1) * gw)
        xo_ref[:, sl] = x[:, sl] + parts[gi] * rs * nw1_ref[:, sl]

    new_hist = ext_ref[L:L + hist, :]
    tail_ref[...] = new_hist
    ext_ref[0:hist, :] = new_hist


def _pool(x, buf, nw0, pw, ps, nw1, *, row0, nseq, T, L, D, pos0, name):
    nc = T // L
    blk0 = row0 // L
    hist = POOL_BUF + 1
    ng = len(POOL_WINDOWS)
    gw = D // ng
    vec = pl.BlockSpec((1, D), lambda b, c: (0, 0))
    xrows = pl.BlockSpec((L, D), lambda b, c: (blk0 + b * nc + c, 0))
    return pl.pallas_call(
        functools.partial(_pool_kernel, L=L, D=D, pos0=pos0),
        grid=(nseq, nc),
        in_specs=[
            xrows,
            pl.BlockSpec((None, hist, D), lambda b, c: (b, 0, 0)),
            vec,
            pl.BlockSpec((ng, gw, gw), lambda b, c: (0, 0, 0)),
            vec, vec,
        ],
        out_specs=[
            xrows,
            pl.BlockSpec((None, hist, D), lambda b, c: (b, 0, 0)),
        ],
        out_shape=[
            jax.ShapeDtypeStruct(x.shape, F32),
            jax.ShapeDtypeStruct((nseq, hist, D), F32),
        ],
        scratch_shapes=[pltpu.VMEM((hist + L, D), F32)],
        input_output_aliases={0: 0},
        compiler_params=_params("parallel", "arbitrary"),
        name=name,
    )(x, buf, nw0, pw, ps, nw1)


def _pad_lanes(v):
    return jnp.pad(v.astype(F32), (0, LANES - v.shape[0])).reshape(1, LANES)


def kernel(x_prompt, x_sample, state_conv, state_ssd, state_hgrn, state_pool, norm_w, w_in,
           conv_w, conv_b, dt_bias, a_log, d_skip, ssd_norm_w, hg_norm_w, hg_lower_bounds,
           w_out, pool_w, pool_scale, w_ffn_up, w_ffn_down):
    B, T, D = x_prompt.shape
    Bs, Ts, _ = x_sample.shape
    depth = norm_w.shape[0]
    heads = dt_bias.shape[1]
    P = D // heads
    conv_dim = conv_w.shape[-1]
    G = (conv_dim - D) // (2 * SSD_STATE)
    assert conv_dim == 2 * D and LANES % P == 0 and heads <= LANES and D % (G * LANES) == 0
    assert Ts >= POOL_BUF + 1 and T >= POOL_BUF + 1
    Mp, Ms = B * T, Bs * Ts
    npairs = D // LANES
    H = D // HG_KEY
    hist = POOL_BUF + 1
    keep = SSD_CONV - 1

    segs = (
        dict(row0=0, nseq=B, T=T),
        dict(row0=Mp, nseq=Bs, T=Ts),
    )

    x = jnp.concatenate([x_prompt.reshape(Mp, D), x_sample.reshape(Ms, D)], axis=0)

    lbs = jnp.cumsum(jax.nn.softmax(hg_lower_bounds.astype(F32), axis=0), axis=0)
    lbs = lbs - lbs[0]

    conv_out, ssd_out, hg_out, pool_out = ([], []), ([], []), ([], []), ([], [])
    for layer in range(depth):
        j = layer // 2
        nw = norm_w[layer].astype(F32)
        if layer % 2 == 0:
            wj = w_in[j]
            w_main = jnp.concatenate([wj[:, :3 * D], wj[:, 3 * D + heads:]], axis=1).astype(BF16)
            w_dt = jnp.pad(wj[:, 3 * D:3 * D + heads], ((0, 0), (0, LANES - heads))).astype(BF16)
            proj, dtr = _norm_mm(x, nw[0], w_main, w_dt, name=f"in_proj_{layer}")
            dsk = jnp.repeat(d_skip[j].astype(F32), P).reshape(1, D)
            mixed = None
            for si, seg in enumerate(segs):
                nseq, tt = seg["nseq"], seg["T"]
                if si == 0:
                    conv0 = jnp.zeros((nseq, SUBLANES, conv_dim), F32)
                    h0 = jnp.zeros((nseq, npairs, LANES, SSD_STATE), F32)
                else:
                    conv0 = jnp.pad(state_conv[j].astype(F32),
                                    ((0, 0), (SUBLANES - keep, 0), (0, 0)))
                    h0 = state_ssd[j].astype(F32).reshape(nseq, npairs, LANES, SSD_STATE)
                mixed, hfin, craw = _ssd(
                    proj, dtr, conv0, h0, conv_w[j].astype(F32),
                    conv_b[j].astype(F32).reshape(1, conv_dim), _pad_lanes(dt_bias[j]),
                    _pad_lanes(a_log[j]), dsk, ssd_norm_w[j].astype(F32).reshape(1, D), mixed,
                    L=_tile(tt, (128, 64, 32, 16)), D=D, G=G, P=P, name=f"ssd_{layer}_{si}", **seg)
                conv_out[si].append(craw[:, SUBLANES - keep:, :])
                ssd_out[si].append(hfin.reshape(nseq, heads, P, SSD_STATE))
            for si, seg in enumerate(segs):
                nseq, tt = seg["nseq"], seg["T"]
                if si == 0:
                    s0t = jnp.zeros((nseq, H, HG_KEY, HG_KEY), F32)
                else:
                    s0t = jnp.swapaxes(state_hgrn[j].astype(F32), -1, -2)
                mixed, sfin = _hgrn(proj, lbs[j].reshape(1, D), s0t,
                                    hg_norm_w[j].astype(F32).reshape(1, D), mixed,
                                    L=_tile(tt, (64, 32, 16)), D=D, name=f"hgrn_{layer}_{si}", **seg)
                hg_out[si].append(jnp.swapaxes(sfin, -1, -2))
            x = _mm_res_norm(mixed, w_out[j].astype(BF16), x, nw[1], name=f"out_proj_{layer}")
        else:
            for si, seg in enumerate(segs):
                nseq, tt = seg["nseq"], seg["T"]
                if si == 0:
                    buf = jnp.zeros((nseq, hist, D), F32)
                else:
                    buf = jnp.pad(state_pool[j].astype(F32), ((0, 0), (1, 0), (0, 0)))
                x, tail = _pool(x, buf, nw[0].reshape(1, D), pool_w[j].astype(BF16),
                                pool_scale[j].astype(F32).reshape(1, D), nw[1].reshape(1, D),
                                L=_tile(tt, (256, 128, 64, 32, 16)), D=D,
                                pos0=0 if si == 0 else PAST_LEN, name=f"pool_{layer}_{si}", **seg)
                pool_out[si].append(tail[:, 1:, :])
        x = _ffn(x, nw[2], w_ffn_up[layer].astype(BF16), w_ffn_down[layer].astype(BF16), nw[3],
                 name=f"ffn_{layer}")

    dt_out = x_prompt.dtype
    y_prompt = x[:Mp].reshape(B, T, D).astype(dt_out)
    y_sample = x[Mp:].reshape(Bs, Ts, D).astype(dt_out)

    def stack(lst):
        return jnp.stack(lst).astype(dt_out)

    return (y_prompt, y_sample,
            stack(conv_out[0]), stack(ssd_out[0]), stack(hg_out[0]), stack(pool_out[0]),
            stack(conv_out[1]), stack(ssd_out[1]), stack(hg_out[1]), stack(pool_out[1]))
```

```python
import functools

import jax
import jax.numpy as jnp
from jax import lax
from jax.experimental import pallas as pl
from jax.experimental.pallas import tpu as pltpu

F32 = jnp.float32
BF16 = jnp.bfloat16
EPS = 1e-6
PAST_LEN = 4096
POOL_WINDOWS = (2, 4, 8, 16)
POOL_BUF = max(POOL_WINDOWS) - 1
SSD_STATE = 128
SSD_CONV = 4
HG_KEY = 128
LOG2_E = 1.4426950408889634
LANES = 128
SUBLANES = 8
VMEM_LIMIT_BYTES = 56 * 1024 * 1024

_NT = (((1,), (1,)), ((), ()))
_TN = (((0,), (0,)), ((), ()))


def _tile(n, candidates):
    for c in candidates:
        if n % c == 0:
            return c
    raise ValueError(f"no tile for {n} in {candidates}")


def _params(*sem):
    return pltpu.CompilerParams(dimension_semantics=sem, vmem_limit_bytes=VMEM_LIMIT_BYTES)


def _softplus(x):
    return jnp.maximum(x, 0.0) + jnp.log(1.0 + jnp.exp(-jnp.abs(x)))


def _sigmoid(x):
    return 0.5 * jnp.tanh(0.5 * x) + 0.5


def _silu(x):
    h = 0.5 * x
    return h * jnp.tanh(h) + h


def _roll_rows(x, shift):
    return pltpu.roll(x, shift, 0)


def _roll_in_block(x, shift):
    rows, width = x.shape
    x3 = x.reshape(rows // SUBLANES, SUBLANES, width)
    return pltpu.roll(x3, shift, 1).reshape(rows, width)


ROW_TILES = (1280, 1024, 512, 256, 192, 128, 64, 32, 16)
ROW_CHUNK = 640
ACC_COLS = 512


def _row_chunks(tm):
    rc = ROW_CHUNK if tm % ROW_CHUNK == 0 else tm
    return [slice(r0, r0 + rc) for r0 in range(0, tm, rc)]


def _rms_rows(x, w):
    ms = jnp.mean(x * x, axis=-1, keepdims=True)
    return x * lax.rsqrt(ms + EPS) * w


def _accumulate(acc_ref, rows, lhs, w_ref):
    n = acc_ref.shape[1]
    step = min(ACC_COLS, n)
    for n0 in range(0, n, step):
        cols = slice(n0, n0 + step)
        acc_ref[rows, cols] += jnp.dot(lhs, w_ref[:, cols], preferred_element_type=F32)


def _norm_mm_kernel(x_ref, nw_ref, w_ref, w2_ref, o_ref, o2_ref, hn_ref):
    chunks = _row_chunks(x_ref.shape[0])

    @pl.when(pl.program_id(1) == 0)
    def _():
        for rows in chunks:
            hn_ref[rows, :] = _rms_rows(x_ref[rows, :], nw_ref[...]).astype(BF16)
            o2_ref[rows, :] = jnp.dot(hn_ref[rows, :], w2_ref[...], preferred_element_type=F32)

    for rows in chunks:
        o_ref[rows, :] = jnp.dot(hn_ref[rows, :], w_ref[...],
                                 preferred_element_type=F32).astype(o_ref.dtype)


def _norm_mm(x, nw, w, w2, *, name):
    M, K = x.shape
    N = w.shape[1]
    tm = _tile(M, ROW_TILES)
    tn = _tile(N, (1024, 512, 256, 128))
    return pl.pallas_call(
        _norm_mm_kernel,
        grid=(M // tm, N // tn),
        in_specs=[
            pl.BlockSpec((tm, K), lambda i, j: (i, 0)),
            pl.BlockSpec((1, K), lambda i, j: (0, 0)),
            pl.BlockSpec((K, tn), lambda i, j: (0, j)),
            pl.BlockSpec((K, LANES), lambda i, j: (0, 0)),
        ],
        out_specs=[
            pl.BlockSpec((tm, tn), lambda i, j: (i, j)),
            pl.BlockSpec((tm, LANES), lambda i, j: (i, 0)),
        ],
        out_shape=[
            jax.ShapeDtypeStruct((M, N), F32),
            jax.ShapeDtypeStruct((M, LANES), F32),
        ],
        scratch_shapes=[pltpu.VMEM((tm, K), BF16)],
        compiler_params=_params("parallel", "arbitrary"),
        name=name,
    )(x, nw.reshape(1, K), w, w2)


def _mm_res_norm_kernel(x_ref, w_ref, r_ref, nw_ref, o_ref, *, nk):
    k = pl.program_id(1)
    chunks = _row_chunks(x_ref.shape[0])

    @pl.when(k == 0)
    def _():
        o_ref[...] = jnp.zeros_like(o_ref)

    for rows in chunks:
        _accumulate(o_ref, rows, x_ref[rows, :], w_ref)

    @pl.when(k == nk - 1)
    def _():
        for rows in chunks:
            o_ref[rows, :] = r_ref[rows, :] + _rms_rows(o_ref[rows, :], nw_ref[...])


def _mm_res_norm(x, w, resid, nw, *, name):
    M, K = x.shape
    N = w.shape[1]
    tm = _tile(M, ROW_TILES)
    tk = _tile(K, (1024, 512, 256, 128))
    nk = K // tk
    return pl.pallas_call(
        functools.partial(_mm_res_norm_kernel, nk=nk),
        grid=(M // tm, nk),
        in_specs=[
            pl.BlockSpec((tm, tk), lambda i, k: (i, k)),
            pl.BlockSpec((tk, N), lambda i, k: (k, 0)),
            pl.BlockSpec((tm, N), lambda i, k: (i, 0)),
            pl.BlockSpec((1, N), lambda i, k: (0, 0)),
        ],
        out_specs=pl.BlockSpec((tm, N), lambda i, k: (i, 0), pipeline_mode=pl.Buffered(1)),
        out_shape=jax.ShapeDtypeStruct((M, N), F32),
        compiler_params=_params("parallel", "arbitrary"),
        name=name,
    )(x, w, resid, nw.reshape(1, N))


def _ffn_kernel(x_ref, nw_in_ref, wu_ref, wd_ref, nw_out_ref, o_ref, hn_ref, *, nf):
    f = pl.program_id(1)
    chunks = _row_chunks(x_ref.shape[0])

    @pl.when(f == 0)
    def _():
        o_ref[...] = jnp.zeros_like(o_ref)
        for rows in chunks:
            hn_ref[rows, :] = _rms_rows(x_ref[rows, :], nw_in_ref[...]).astype(BF16)

    for rows in chunks:
        h = jnp.dot(hn_ref[rows, :], wu_ref[...], preferred_element_type=F32)
        h = jnp.square(jnp.maximum(h, 0.0)).astype(BF16)
        _accumulate(o_ref, rows, h, wd_ref)

    @pl.when(f == nf - 1)
    def _():
        for rows in chunks:
            o_ref[rows, :] = x_ref[rows, :] + _rms_rows(o_ref[rows, :], nw_out_ref[...])


def _ffn(x, nw_in, wu, wd, nw_out, *, row0=0, nrows=None, name):
    D = x.shape[1]
    M = x.shape[0] if nrows is None else nrows
    F = wu.shape[1]
    tm = _tile(M, tuple(c for c in ROW_TILES if row0 % c == 0))
    blk0 = row0 // tm
    tf = _tile(F, (512, 256, 128))
    nf = F // tf
    vec = pl.BlockSpec((1, D), lambda i, f: (0, 0))
    return pl.pallas_call(
        functools.partial(_ffn_kernel, nf=nf),
        grid=(M // tm, nf),
        in_specs=[
            pl.BlockSpec((tm, D), lambda i, f: (blk0 + i, 0)),
            vec,
            pl.BlockSpec((D, tf), lambda i, f: (0, f)),
            pl.BlockSpec((tf, D), lambda i, f: (f, 0)),
            vec,
        ],
        out_specs=pl.BlockSpec((tm, D), lambda i, f: (i, 0), pipeline_mode=pl.Buffered(1)),
        out_shape=jax.ShapeDtypeStruct((M, D), F32),
        scratch_shapes=[pltpu.VMEM((tm, D), BF16)],
        compiler_params=_params("parallel", "arbitrary"),
        name=name,
    )(x, nw_in.reshape(1, D), wu, wd, nw_out.reshape(1, D))


def _transpose_exact(a, eye):
    out = None
    r = a
    for _ in range(3):
        p = r.astype(BF16)
        r = r - p.astype(F32)
        t = lax.dot_general(eye, p, _NT, preferred_element_type=F32)
        out = t if out is None else out + t
    return out


def _ssd_kernel(*refs, L, D, G, P, aliased):
    (z_ref, xr_ref, bcr_ref, dtr_ref, conv0_ref, h0_ref, cwx_ref, cwbc_ref,
     cbx_ref, cbbc_ref, dtb_ref, alog_ref, dsk_ref, nw_ref, eye_ref) = refs[:15]
    y_ref, hout_ref, cout_ref, ext_ref, h_ref, yscr_ref = refs[15 + int(aliased):]
    c = pl.program_id(1)
    nc = pl.num_programs(1)
    N = SSD_STATE
    npairs = D // LANES
    pairs_per_group = npairs // G
    heads_per_pair = LANES // P

    @pl.when(c == 0)
    def _():
        ext_ref[0:SUBLANES, :] = conv0_ref[...]
        h_ref[...] = h0_ref[...]

    ext_ref[SUBLANES:SUBLANES + L, 0:D] = xr_ref[...]
    ext_ref[SUBLANES:SUBLANES + L, D:2 * D] = bcr_ref[...]

    rsub = lax.broadcasted_iota(jnp.int32, (L, D), 0) & (SUBLANES - 1)

    def conv(lo, w_ref, b_ref):
        u = ext_ref[:, lo:lo + D]
        acc = u[SUBLANES:] * w_ref[SSD_CONV - 1:SSD_CONV, :]
        for s in range(1, SSD_CONV):
            r = _roll_in_block(u, s)
            shifted = jnp.where(rsub >= s, r[SUBLANES:], r[:L])
            acc = acc + shifted * w_ref[SSD_CONV - 1 - s:SSD_CONV - s, :]
        return _silu(acc + b_ref[...])

    xs = conv(0, cwx_ref, cbx_ref)
    bc = conv(D, cwbc_ref, cbbc_ref)
    last_rows = ext_ref[L:L + SUBLANES, :]
    ext_ref[0:SUBLANES, :] = last_rows

    dt = _softplus(dtr_ref[...] + dtb_ref[...])
    a = dt * (-jnp.exp(alog_ref[...]))
    row = lax.broadcasted_iota(jnp.int32, (L, LANES), 0)
    acs = a
    s = 1
    while s < L:
        acs = acs + jnp.where(row >= s, _roll_rows(acs, s), 0.0)
        s *= 2
    acs_t = _transpose_exact(acs, eye_ref[...])
    last = acs[L - 1:L, :]
    e_in = jnp.exp(acs)
    e_tail = jnp.exp(last - acs)
    e_all = jnp.exp(last)

    lane = lax.broadcasted_iota(jnp.int32, (L, LANES), 1)
    tri = (lax.broadcasted_iota(jnp.int32, (L, L), 0)
           >= lax.broadcasted_iota(jnp.int32, (L, L), 1))
    srow = lax.broadcasted_iota(jnp.int32, (LANES, LANES), 0)

    def per_head(arr, j, rows=lane):
        out = None
        for hh in reversed(range(heads_per_pair)):
            h = j * heads_per_pair + hh
            col = arr[:, h:h + 1]
            out = col if out is None else jnp.where(rows < (hh + 1) * P, col, out)
        return out

    for g in range(G):
        bg = bc[:, g * N:(g + 1) * N].astype(BF16)
        cg = bc[:, (G + g) * N:(G + g + 1) * N].astype(BF16)
        cb = lax.dot_general(cg, bg, _NT, preferred_element_type=F32)
        for jj in range(pairs_per_group):
            j = g * pairs_per_group + jj
            sl = slice(j * LANES, (j + 1) * LANES)
            dx = per_head(dt, j) * xs[:, sl]
            dxb = dx.astype(BF16)
            ydiag = None
            for hh in reversed(range(heads_per_pair)):
                h = j * heads_per_pair + hh
                seg = jnp.where(tri, acs[:, h:h + 1] - acs_t[h:h + 1, :], -jnp.inf)
                m = (cb * jnp.exp(seg)).astype(BF16)
                yh = jnp.dot(m, dxb, preferred_element_type=F32)
                ydiag = yh if ydiag is None else jnp.where(lane < (hh + 1) * P, yh, ydiag)
            hp = h_ref[j]
            ystate = lax.dot_general(cg, hp.astype(BF16), _NT, preferred_element_type=F32)
            yscr_ref[:, sl] = ydiag + ystate * per_head(e_in, j)
            dxw = (dx * per_head(e_tail, j)).astype(BF16)
            upd = lax.dot_general(dxw, bg, _TN, preferred_element_type=F32)
            scale = None
            for hh in reversed(range(heads_per_pair)):
                h = j * heads_per_pair + hh
                col = e_all[:, h:h + 1]
                scale = col if scale is None else jnp.where(srow < (hh + 1) * P, col, scale)
            h_ref[j] = hp * scale + upd

    y = (yscr_ref[...] + dsk_ref[...] * xs) * _silu(z_ref[...])
    gw = D // G
    for g in range(G):
        sl = slice(g * gw, (g + 1) * gw)
        seg = y[:, sl]
        ms = jnp.mean(seg * seg, axis=-1, keepdims=True)
        y_ref[:, sl] = (seg * lax.rsqrt(ms + EPS) * nw_ref[:, sl]).astype(y_ref.dtype)

    @pl.when(c == nc - 1)
    def _():
        hout_ref[...] = h_ref[...]
        cout_ref[...] = last_rows


def _ssd(proj, dtr, conv0, h0, cw, cb, dtb, alog, dsk, nw, mixed, *, row0, nseq, T, L, D, G, P,
         name):
    M = proj.shape[0]
    nc = T // L
    blk0 = row0 // L
    npairs = D // LANES
    aliased = mixed is not None

    def rows(col):
        return pl.BlockSpec((L, D), lambda b, c: (blk0 + b * nc + c, col))

    def whole(shape):
        return pl.BlockSpec(shape, lambda b, c: (0,) * len(shape))

    in_specs = [
        rows(0), rows(1), rows(2),
        pl.BlockSpec((L, LANES), lambda b, c: (blk0 + b * nc + c, 0)),
        pl.BlockSpec((None, SUBLANES, 2 * D), lambda b, c: (b, 0, 0)),
        pl.BlockSpec((None, npairs, LANES, SSD_STATE), lambda b, c: (b, 0, 0, 0)),
        whole((SSD_CONV, D)), whole((SSD_CONV, D)), whole((1, D)), whole((1, D)),
        whole((1, LANES)), whole((1, LANES)), whole((1, D)), whole((1, D)),
        whole((LANES, LANES)),
    ]
    args = [proj, proj, proj, dtr, conv0, h0, cw[:, :D], cw[:, D:], cb[:, :D], cb[:, D:],
            dtb, alog, dsk, nw, jnp.eye(LANES, dtype=BF16)]
    if aliased:
        in_specs.append(pl.BlockSpec(memory_space=pl.ANY))
        args.append(mixed)
    return pl.pallas_call(
        functools.partial(_ssd_kernel, L=L, D=D, G=G, P=P, aliased=aliased),
        grid=(nseq, nc),
        in_specs=in_specs,
        out_specs=[
            pl.BlockSpec((L, D), lambda b, c: (blk0 + b * nc + c, 0)),
            pl.BlockSpec((None, npairs, LANES, SSD_STATE), lambda b, c: (b, 0, 0, 0)),
            pl.BlockSpec((None, SUBLANES, 2 * D), lambda b, c: (b, 0, 0)),
        ],
        out_shape=[
            jax.ShapeDtypeStruct((M, 2 * D), BF16),
            jax.ShapeDtypeStruct((nseq, npairs, LANES, SSD_STATE), F32),
            jax.ShapeDtypeStruct((nseq, SUBLANES, 2 * D), F32),
        ],
        scratch_shapes=[
            pltpu.VMEM((SUBLANES + L, 2 * D), F32),
            pltpu.VMEM((npairs, LANES, SSD_STATE), F32),
            pltpu.VMEM((L, D), F32),
        ],
        input_output_aliases={len(args) - 1: 0} if aliased else {},
        compiler_params=_params("parallel", "arbitrary"),
        name=name,
    )(*args)


def _hgrn_kernel(q_ref, f_ref, i_ref, g_ref, lb_ref, s0_ref, nw_ref, mixed_ref, o_ref, sout_ref,
                 st_ref, *, L, D):
    del mixed_ref
    c = pl.program_id(1)
    nc = pl.num_programs(1)
    H = D // HG_KEY

    @pl.when(c == 0)
    def _():
        st_ref[...] = s0_ref[...]

    lb = lb_ref[...]
    fz = f_ref[...]
    la = jnp.log(lb)
    lg = jnp.log1p(-lb) - (jnp.maximum(-fz, 0.0) + jnp.log(1.0 + jnp.exp(-jnp.abs(fz))))
    logf = jnp.maximum(la, lg) + jnp.log(1.0 + jnp.exp(-jnp.abs(la - lg)))
    q = q_ref[...]
    k = (1.0 - lb) * _sigmoid(-fz)
    v = i_ref[...]
    lf2 = logf * LOG2_E
    f = jnp.exp2(lf2)

    row = lax.broadcasted_iota(jnp.int32, (L, D), 0)
    r8 = row & (SUBLANES - 1)
    p = lf2
    t = lf2
    for s in (1, 2, 4):
        p = p + jnp.where(r8 >= s, _roll_in_block(p, s), 0.0)
        t = t + _roll_in_block(t, s)
    levels = []
    m = SUBLANES
    while m < L:
        second = (row & (2 * m - 1)) >= m
        ex = jnp.exp2(jnp.where(second, p, t - p))
        levels.append((m, (q * ex).astype(BF16), (k * ex).astype(BF16)))
        prev_t = _roll_rows(t, m)
        next_t = _roll_rows(t, L - m)
        p = p + jnp.where(second, prev_t, 0.0)
        t = t + jnp.where(second, prev_t, next_t)
        m *= 2
    q_in = (q * jnp.exp2(p)).astype(BF16)
    k_tail = (k * jnp.exp2(t - p)).astype(BF16)
    dec = jnp.exp2(t[0:1, :])
    vb = v.astype(BF16)

    rl = lax.broadcasted_iota(jnp.int32, (L, L), 0)
    cl = lax.broadcasted_iota(jnp.int32, (L, L), 1)
    lag_masks = [(cl == rl - d) & ((rl & (SUBLANES - 1)) >= d) for d in range(SUBLANES)]
    level_masks = [((rl >> ((2 * m).bit_length() - 1)) == (cl >> ((2 * m).bit_length() - 1)))
                   & ((rl & (2 * m - 1)) >= m) & ((cl & (2 * m - 1)) < m)
                   for (m, _, _) in levels]

    for h in range(H):
        sl = slice(h * HG_KEY, (h + 1) * HG_KEY)
        qh, kh, fh = q[:, sl], k[:, sl], f[:, sl]
        amat = jnp.where(lag_masks[0], jnp.sum(qh * kh, axis=-1, keepdims=True), 0.0)
        fd = None
        for d in range(1, SUBLANES):
            sh = fh if d == 1 else _roll_in_block(fh, d - 1)
            fd = sh if fd is None else fd * sh
            w = qh * fd * _roll_in_block(kh, d)
            amat = jnp.where(lag_masks[d], jnp.sum(w, axis=-1, keepdims=True), amat)
        for (m, ql, kl), mask in zip(levels, level_masks):
            sc = lax.dot_general(ql[:, sl], kl[:, sl], _NT, preferred_element_type=F32)
            amat = jnp.where(mask, sc, amat)
        o = jnp.dot(amat.astype(BF16), vb[:, sl], preferred_element_type=F32)
        st = st_ref[h]
        o = o + lax.dot_general(q_in[:, sl], st.astype(BF16), _NT, preferred_element_type=F32)
        st_ref[h] = st * dec[:, sl] + lax.dot_general(vb[:, sl], k_tail[:, sl], _TN,
                                                      preferred_element_type=F32)
        ms = jnp.mean(o * o, axis=-1, keepdims=True)
        o = o * lax.rsqrt(ms + EPS)
        o_ref[:, sl] = (o * nw_ref[:, sl] * _silu(g_ref[:, sl])).astype(o_ref.dtype)

    @pl.when(c == nc - 1)
    def _():
        sout_ref[...] = st_ref[...]


def _hgrn(proj, lb, s0t, nw, mixed, *, row0, nseq, T, L, D, name):
    nc = T // L
    blk0 = row0 // L
    H = D // HG_KEY

    def rows(col):
        return pl.BlockSpec((L, D), lambda b, c: (blk0 + b * nc + c, col))

    return pl.pallas_call(
        functools.partial(_hgrn_kernel, L=L, D=D),
        grid=(nseq, nc),
        in_specs=[
            rows(3), rows(4), rows(5), rows(6),
            pl.BlockSpec((1, D), lambda b, c: (0, 0)),
            pl.BlockSpec((None, H, HG_KEY, HG_KEY), lambda b, c: (b, 0, 0, 0)),
            pl.BlockSpec((1, D), lambda b, c: (0, 0)),
            pl.BlockSpec(memory_space=pl.ANY),
        ],
        out_specs=[
            rows(1),
            pl.BlockSpec((None, H, HG_KEY, HG_KEY), lambda b, c: (b, 0, 0, 0)),
        ],
        out_shape=[
            jax.ShapeDtypeStruct(mixed.shape, mixed.dtype),
            jax.ShapeDtypeStruct((nseq, H, HG_KEY, HG_KEY), F32),
        ],
        scratch_shapes=[pltpu.VMEM((H, HG_KEY, HG_KEY), F32)],
        input_output_aliases={7: 0},
        compiler_params=_params("parallel", "arbitrary"),
        name=name,
    )(proj, proj, proj, proj, lb, s0t, nw, mixed)


def _pool_kernel(x_ref, buf_ref, nw0_ref, pw_ref, ps_ref, nw1_ref, xo_ref, tail_ref, ext_ref,
                 *, L, D, pos0):
    c = pl.program_id(1)
    hist = POOL_BUF + 1

    @pl.when(c == 0)
    def _():
        ext_ref[0:hist, :] = buf_ref[...]

    x = x_ref[...]
    ms = jnp.mean(x * x, axis=-1, keepdims=True)
    hn = x * lax.rsqrt(ms + EPS) * nw0_ref[...]
    ext_ref[hist:hist + L, :] = hn

    pos = pos0 + c * L + lax.broadcasted_iota(jnp.int32, (L, 1), 0)
    gw = D // len(POOL_WINDOWS)
    parts = []
    ss = None
    for gi, w in enumerate(POOL_WINDOWS):
        sl = slice(gi * gw, (gi + 1) * gw)
        s = ext_ref[:, sl]
        span = 1
        while span < w:
            s = s + _roll_rows(s, span)
            span *= 2
        cnt = jnp.minimum(pos + 1, w).astype(F32)
        pooled = s[hist:hist + L] / cnt - hn[:, sl]
        mixed = jnp.dot(pooled.astype(BF16), pw_ref[gi], preferred_element_type=F32) * ps_ref[:, sl]
        parts.append(mixed)
        sq = jnp.sum(mixed * mixed, axis=-1, keepdims=True)
        ss = sq if ss is None else ss + sq
    rs = lax.rsqrt(ss / D + EPS)
    for gi in range(len(POOL_WINDOWS)):
        sl = slice(gi * gw, (gi + 1) * gw)
        xo_ref[:, sl] = x[:, sl] + parts[gi] * rs * nw1_ref[:, sl]

    new_hist = ext_ref[L:L + hist, :]
    tail_ref[...] = new_hist
    ext_ref[0:hist, :] = new_hist


def _pool(x, buf, nw0, pw, ps, nw1, *, row0, nseq, T, L, D, pos0, name):
    nc = T // L
    blk0 = row0 // L
    hist = POOL_BUF + 1
    ng = len(POOL_WINDOWS)
    gw = D // ng
    vec = pl.BlockSpec((1, D), lambda b, c: (0, 0))
    xrows = pl.BlockSpec((L, D), lambda b, c: (blk0 + b * nc + c, 0))
    return pl.pallas_call(
        functools.partial(_pool_kernel, L=L, D=D, pos0=pos0),
        grid=(nseq, nc),
        in_specs=[
            xrows,
            pl.BlockSpec((None, hist, D), lambda b, c: (b, 0, 0)),
            vec,
            pl.BlockSpec((ng, gw, gw), lambda b, c: (0, 0, 0)),
            vec, vec,
        ],
        out_specs=[
            xrows,
            pl.BlockSpec((None, hist, D), lambda b, c: (b, 0, 0)),
        ],
        out_shape=[
            jax.ShapeDtypeStruct(x.shape, F32),
            jax.ShapeDtypeStruct((nseq, hist, D), F32),
        ],
        scratch_shapes=[pltpu.VMEM((hist + L, D), F32)],
        input_output_aliases={0: 0},
        compiler_params=_params("parallel", "arbitrary"),
        name=name,
    )(x, buf, nw0, pw, ps, nw1)


def _pad_lanes(v):
    return jnp.pad(v.astype(F32), (0, LANES - v.shape[0])).reshape(1, LANES)


def kernel(x_prompt, x_sample, state_conv, state_ssd, state_hgrn, state_pool, norm_w, w_in,
           conv_w, conv_b, dt_bias, a_log, d_skip, ssd_norm_w, hg_norm_w, hg_lower_bounds,
           w_out, pool_w, pool_scale, w_ffn_up, w_ffn_down):
    B, T, D = x_prompt.shape
    Bs, Ts, _ = x_sample.shape
    depth = norm_w.shape[0]
    heads = dt_bias.shape[1]
    P = D // heads
    conv_dim = conv_w.shape[-1]
    G = (conv_dim - D) // (2 * SSD_STATE)
    assert conv_dim == 2 * D and LANES % P == 0 and heads <= LANES and D % (G * LANES) == 0
    assert Ts >= POOL_BUF + 1 and T >= POOL_BUF + 1
    Mp, Ms = B * T, Bs * Ts
    npairs = D // LANES
    H = D // HG_KEY
    hist = POOL_BUF + 1
    keep = SSD_CONV - 1

    segs = (
        dict(row0=0, nseq=B, T=T),
        dict(row0=Mp, nseq=Bs, T=Ts),
    )

    x = jnp.concatenate([x_prompt.reshape(Mp, D), x_sample.reshape(Ms, D)], axis=0)

    lbs = jnp.cumsum(jax.nn.softmax(hg_lower_bounds.astype(F32), axis=0), axis=0)
    lbs = lbs - lbs[0]

    conv_out, ssd_out, hg_out, pool_out = ([], []), ([], []), ([], []), ([], [])
    for layer in range(depth):
        j = layer // 2
        nw = norm_w[layer].astype(F32)
        if layer % 2 == 0:
            wj = w_in[j]
            w_main = jnp.concatenate([wj[:, :3 * D], wj[:, 3 * D + heads:]], axis=1).astype(BF16)
            w_dt = jnp.pad(wj[:, 3 * D:3 * D + heads], ((0, 0), (0, LANES - heads))).astype(BF16)
            proj, dtr = _norm_mm(x, nw[0], w_main, w_dt, name=f"in_proj_{layer}")
            dsk = jnp.repeat(d_skip[j].astype(F32), P).reshape(1, D)
            mixed = None
            for si, seg in enumerate(segs):
                nseq, tt = seg["nseq"], seg["T"]
                if si == 0:
                    conv0 = jnp.zeros((nseq, SUBLANES, conv_dim), F32)
                    h0 = jnp.zeros((nseq, npairs, LANES, SSD_STATE), F32)
                else:
                    conv0 = jnp.pad(state_conv[j].astype(F32),
                                    ((0, 0), (SUBLANES - keep, 0), (0, 0)))
                    h0 = state_ssd[j].astype(F32).reshape(nseq, npairs, LANES, SSD_STATE)
                mixed, hfin, craw = _ssd(
                    proj, dtr, conv0, h0, conv_w[j].astype(F32),
                    conv_b[j].astype(F32).reshape(1, conv_dim), _pad_lanes(dt_bias[j]),
                    _pad_lanes(a_log[j]), dsk, ssd_norm_w[j].astype(F32).reshape(1, D), mixed,
                    L=_tile(tt, (128, 64, 32, 16)), D=D, G=G, P=P, name=f"ssd_{layer}_{si}", **seg)
                conv_out[si].append(craw[:, SUBLANES - keep:, :])
                ssd_out[si].append(hfin.reshape(nseq, heads, P, SSD_STATE))
            for si, seg in enumerate(segs):
                nseq, tt = seg["nseq"], seg["T"]
                if si == 0:
                    s0t = jnp.zeros((nseq, H, HG_KEY, HG_KEY), F32)
                else:
                    s0t = jnp.swapaxes(state_hgrn[j].astype(F32), -1, -2)
                mixed, sfin = _hgrn(proj, lbs[j].reshape(1, D), s0t,
                                    hg_norm_w[j].astype(F32).reshape(1, D), mixed,
                                    L=_tile(tt, (64, 32, 16)), D=D, name=f"hgrn_{layer}_{si}", **seg)
                hg_out[si].append(jnp.swapaxes(sfin, -1, -2))
            x = _mm_res_norm(mixed, w_out[j].astype(BF16), x, nw[1], name=f"out_proj_{layer}")
        else:
            for si, seg in enumerate(segs):
                nseq, tt = seg["nseq"], seg["T"]
                if si == 0:
                    buf = jnp.zeros((nseq, hist, D), F32)
                else:
                    buf = jnp.pad(state_pool[j].astype(F32), ((0, 0), (1, 0), (0, 0)))
                x, tail = _pool(x, buf, nw[0].reshape(1, D), pool_w[j].astype(BF16),
                                pool_scale[j].astype(F32).reshape(1, D), nw[1].reshape(1, D),
                                L=_tile(tt, (256, 128, 64, 32, 16)), D=D,
                                pos0=0 if si == 0 else PAST_LEN, name=f"pool_{layer}_{si}", **seg)
                pool_out[si].append(tail[:, 1:, :])
        mlp = functools.partial(_ffn, x, nw[2], w_ffn_up[layer].astype(BF16),
                                w_ffn_down[layer].astype(BF16), nw[3])
        if layer < depth - 1:
            x = mlp(name=f"ffn_{layer}")
        else:
            y_p = mlp(row0=0, nrows=Mp, name=f"ffn_{layer}_0")
            y_s = mlp(row0=Mp, nrows=Ms, name=f"ffn_{layer}_1")

    dt_out = x_prompt.dtype
    y_prompt = y_p.reshape(B, T, D).astype(dt_out)
    y_sample = y_s.reshape(Bs, Ts, D).astype(dt_out)

    def stack(lst):
        return jnp.stack(lst).astype(dt_out)

    return (y_prompt, y_sample,
            stack(conv_out[0]), stack(ssd_out[0]), stack(hg_out[0]), stack(pool_out[0]),
            stack(conv_out[1]), stack(ssd_out[1]), stack(hg_out[1]), stack(pool_out[1]))
```

```python
import functools

import jax
import jax.numpy as jnp
from jax import lax
from jax.experimental import pallas as pl
from jax.experimental.pallas import tpu as pltpu

F32 = jnp.float32
BF16 = jnp.bfloat16
EPS = 1e-6
PAST_LEN = 4096
POOL_WINDOWS = (2, 4, 8, 16)
POOL_BUF = max(POOL_WINDOWS) - 1
SSD_STATE = 128
SSD_CONV = 4
HG_KEY = 128
LOG2_E = 1.4426950408889634
LANES = 128
SUBLANES = 8
VMEM_LIMIT_BYTES = 56 * 1024 * 1024

_NT = (((1,), (1,)), ((), ()))
_TN = (((0,), (0,)), ((), ()))


def _tile(n, candidates):
    for c in candidates:
        if n % c == 0:
            return c
    raise ValueError(f"no tile for {n} in {candidates}")


def _params(*sem):
    return pltpu.CompilerParams(dimension_semantics=sem, vmem_limit_bytes=VMEM_LIMIT_BYTES)


def _softplus(x):
    return jnp.maximum(x, 0.0) + jnp.log(1.0 + jnp.exp(-jnp.abs(x)))


def _sigmoid(x):
    return 0.5 * jnp.tanh(0.5 * x) + 0.5


def _silu(x):
    h = 0.5 * x
    return h * jnp.tanh(h) + h


def _roll_rows(x, shift):
    return pltpu.roll(x, shift, 0)


def _roll_in_block(x, shift):
    rows, width = x.shape
    x3 = x.reshape(rows // SUBLANES, SUBLANES, width)
    return pltpu.roll(x3, shift, 1).reshape(rows, width)


ROW_TILES = (1280, 1024, 512, 256, 192, 128, 64, 32, 16)
ROW_CHUNK = 640
ACC_COLS = 512


def _row_chunks(tm):
    rc = ROW_CHUNK if tm % ROW_CHUNK == 0 else tm
    return [slice(r0, r0 + rc) for r0 in range(0, tm, rc)]


def _rms_rows(x, w):
    ms = jnp.mean(x * x, axis=-1, keepdims=True)
    return x * lax.rsqrt(ms + EPS) * w


def _accumulate(acc_ref, rows, lhs, w_ref):
    n = acc_ref.shape[1]
    step = min(ACC_COLS, n)
    for n0 in range(0, n, step):
        cols = slice(n0, n0 + step)
        acc_ref[rows, cols] += jnp.dot(lhs, w_ref[:, cols], preferred_element_type=F32)


def _norm_mm_kernel(x_ref, nw_ref, w_ref, w2_ref, o_ref, o2_ref, hn_ref):
    chunks = _row_chunks(x_ref.shape[0])

    @pl.when(pl.program_id(1) == 0)
    def _():
        for rows in chunks:
            hn_ref[rows, :] = _rms_rows(x_ref[rows, :], nw_ref[...]).astype(BF16)
            o2_ref[rows, :] = jnp.dot(hn_ref[rows, :], w2_ref[...], preferred_element_type=F32)

    for rows in chunks:
        o_ref[rows, :] = jnp.dot(hn_ref[rows, :], w_ref[...],
                                 preferred_element_type=F32).astype(o_ref.dtype)


def _norm_mm(x, nw, w, w2, *, name):
    M, K = x.shape
    N = w.shape[1]
    tm = _tile(M, ROW_TILES)
    tn = _tile(N, (1024, 512, 256, 128))
    return pl.pallas_call(
        _norm_mm_kernel,
        grid=(M // tm, N // tn),
        in_specs=[
            pl.BlockSpec((tm, K), lambda i, j: (i, 0)),
            pl.BlockSpec((1, K), lambda i, j: (0, 0)),
            pl.BlockSpec((K, tn), lambda i, j: (0, j)),
            pl.BlockSpec((K, LANES), lambda i, j: (0, 0)),
        ],
        out_specs=[
            pl.BlockSpec((tm, tn), lambda i, j: (i, j)),
            pl.BlockSpec((tm, LANES), lambda i, j: (i, 0)),
        ],
        out_shape=[
            jax.ShapeDtypeStruct((M, N), F32),
            jax.ShapeDtypeStruct((M, LANES), F32),
        ],
        scratch_shapes=[pltpu.VMEM((tm, K), BF16)],
        compiler_params=_params("parallel", "arbitrary"),
        name=name,
    )(x, nw.reshape(1, K), w, w2)


def _mm_res_norm_kernel(x_ref, w_ref, r_ref, nw_ref, o_ref, *, nk):
    k = pl.program_id(1)
    chunks = _row_chunks(x_ref.shape[0])

    @pl.when(k == 0)
    def _():
        o_ref[...] = jnp.zeros_like(o_ref)

    for rows in chunks:
        _accumulate(o_ref, rows, x_ref[rows, :], w_ref)

    @pl.when(k == nk - 1)
    def _():
        for rows in chunks:
            o_ref[rows, :] = r_ref[rows, :] + _rms_rows(o_ref[rows, :], nw_ref[...])


def _mm_res_norm(x, w, resid, nw, *, name):
    M, K = x.shape
    N = w.shape[1]
    tm = _tile(M, ROW_TILES)
    tk = _tile(K, (1024, 512, 256, 128))
    nk = K // tk
    return pl.pallas_call(
        functools.partial(_mm_res_norm_kernel, nk=nk),
        grid=(M // tm, nk),
        in_specs=[
            pl.BlockSpec((tm, tk), lambda i, k: (i, k)),
            pl.BlockSpec((tk, N), lambda i, k: (k, 0)),
            pl.BlockSpec((tm, N), lambda i, k: (i, 0)),
            pl.BlockSpec((1, N), lambda i, k: (0, 0)),
        ],
        out_specs=pl.BlockSpec((tm, N), lambda i, k: (i, 0), pipeline_mode=pl.Buffered(1)),
        out_shape=jax.ShapeDtypeStruct((M, N), F32),
        compiler_params=_params("parallel", "arbitrary"),
        name=name,
    )(x, w, resid, nw.reshape(1, N))


def _ffn_kernel(x_ref, nw_in_ref, wu_ref, wd_ref, nw_out_ref, o_ref, hn_ref, *, nf):
    f = pl.program_id(1)
    chunks = _row_chunks(x_ref.shape[0])

    @pl.when(f == 0)
    def _():
        o_ref[...] = jnp.zeros_like(o_ref)
        for rows in chunks:
            hn_ref[rows, :] = _rms_rows(x_ref[rows, :], nw_in_ref[...]).astype(BF16)

    for rows in chunks:
        h = jnp.dot(hn_ref[rows, :], wu_ref[...], preferred_element_type=F32)
        h = jnp.square(jnp.maximum(h, 0.0)).astype(BF16)
        _accumulate(o_ref, rows, h, wd_ref)

    @pl.when(f == nf - 1)
    def _():
        for rows in chunks:
            o_ref[rows, :] = x_ref[rows, :] + _rms_rows(o_ref[rows, :], nw_out_ref[...])


def _ffn(x, nw_in, wu, wd, nw_out, *, row0=0, nrows=None, name):
    D = x.shape[1]
    M = x.shape[0] if nrows is None else nrows
    F = wu.shape[1]
    tm = _tile(M, tuple(c for c in ROW_TILES if row0 % c == 0))
    blk0 = row0 // tm
    tf = _tile(F, (512, 256, 128))
    nf = F // tf
    vec = pl.BlockSpec((1, D), lambda i, f: (0, 0))
    return pl.pallas_call(
        functools.partial(_ffn_kernel, nf=nf),
        grid=(M // tm, nf),
        in_specs=[
            pl.BlockSpec((tm, D), lambda i, f: (blk0 + i, 0)),
            vec,
            pl.BlockSpec((D, tf), lambda i, f: (0, f)),
            pl.BlockSpec((tf, D), lambda i, f: (f, 0)),
            vec,
        ],
        out_specs=pl.BlockSpec((tm, D), lambda i, f: (i, 0), pipeline_mode=pl.Buffered(1)),
        out_shape=jax.ShapeDtypeStruct((M, D), F32),
        scratch_shapes=[pltpu.VMEM((tm, D), BF16)],
        compiler_params=_params("parallel", "arbitrary"),
        name=name,
    )(x, nw_in.reshape(1, D), wu, wd, nw_out.reshape(1, D))


def _transpose_exact(a, eye):
    out = None
    r = a
    for _ in range(3):
        p = r.astype(BF16)
        r = r - p.astype(F32)
        t = lax.dot_general(eye, p, _NT, preferred_element_type=F32)
        out = t if out is None else out + t
    return out


def _ssd_kernel(*refs, L, D, G, P, aliased):
    (z_ref, xr_ref, bcr_ref, dtr_ref, conv0_ref, h0_ref, cwx_ref, cwbc_ref,
     cbx_ref, cbbc_ref, dtb_ref, alog_ref, dsk_ref, nw_ref, eye_ref) = refs[:15]
    y_ref, hout_ref, cout_ref, ext_ref, h_ref, yscr_ref = refs[15 + int(aliased):]
    c = pl.program_id(1)
    nc = pl.num_programs(1)
    N = SSD_STATE
    npairs = D // LANES
    pairs_per_group = npairs // G
    heads_per_pair = LANES // P

    @pl.when(c == 0)
    def _():
        ext_ref[0:SUBLANES, :] = conv0_ref[...]
        h_ref[...] = h0_ref[...]

    ext_ref[SUBLANES:SUBLANES + L, 0:D] = xr_ref[...]
    ext_ref[SUBLANES:SUBLANES + L, D:2 * D] = bcr_ref[...]

    rsub = lax.broadcasted_iota(jnp.int32, (L, D), 0) & (SUBLANES - 1)

    def conv(lo, w_ref, b_ref):
        u = ext_ref[:, lo:lo + D]
        acc = u[SUBLANES:] * w_ref[SSD_CONV - 1:SSD_CONV, :]
        for s in range(1, SSD_CONV):
            r = _roll_in_block(u, s)
            shifted = jnp.where(rsub >= s, r[SUBLANES:], r[:L])
            acc = acc + shifted * w_ref[SSD_CONV - 1 - s:SSD_CONV - s, :]
        return _silu(acc + b_ref[...])

    xs = conv(0, cwx_ref, cbx_ref)
    bc = conv(D, cwbc_ref, cbbc_ref)
    last_rows = ext_ref[L:L + SUBLANES, :]
    ext_ref[0:SUBLANES, :] = last_rows

    dt = _softplus(dtr_ref[...] + dtb_ref[...])
    a = dt * (-LOG2_E * jnp.exp(alog_ref[...]))
    row = lax.broadcasted_iota(jnp.int32, (L, LANES), 0)
    acs = a
    s = 1
    while s < L:
        acs = acs + jnp.where(row >= s, _roll_rows(acs, s), 0.0)
        s *= 2
    acs_t = _transpose_exact(acs, eye_ref[...])
    last = acs[L - 1:L, :]
    e_in = jnp.exp2(acs)
    e_tail = jnp.exp2(last - acs)
    e_all = jnp.exp2(last)

    lane = lax.broadcasted_iota(jnp.int32, (L, LANES), 1)
    tri = (lax.broadcasted_iota(jnp.int32, (L, L), 0)
           >= lax.broadcasted_iota(jnp.int32, (L, L), 1))
    srow = lax.broadcasted_iota(jnp.int32, (LANES, LANES), 0)

    def per_head(arr, j, rows=lane):
        out = None
        for hh in reversed(range(heads_per_pair)):
            h = j * heads_per_pair + hh
            col = arr[:, h:h + 1]
            out = col if out is None else jnp.where(rows < (hh + 1) * P, col, out)
        return out

    for g in range(G):
        bg = bc[:, g * N:(g + 1) * N].astype(BF16)
        cg = bc[:, (G + g) * N:(G + g + 1) * N].astype(BF16)
        cb = lax.dot_general(cg, bg, _NT, preferred_element_type=F32)
        cb = jnp.where(tri, cb, 0.0)
        for jj in range(pairs_per_group):
            j = g * pairs_per_group + jj
            sl = slice(j * LANES, (j + 1) * LANES)
            dx = per_head(dt, j) * xs[:, sl]
            dxb = dx.astype(BF16)
            ydiag = None
            for hh in reversed(range(heads_per_pair)):
                h = j * heads_per_pair + hh
                seg = jnp.minimum(acs[:, h:h + 1] - acs_t[h:h + 1, :], 0.0)
                m = (cb * jnp.exp2(seg)).astype(BF16)
                yh = jnp.dot(m, dxb, preferred_element_type=F32)
                ydiag = yh if ydiag is None else jnp.where(lane < (hh + 1) * P, yh, ydiag)
            hp = h_ref[j]
            ystate = lax.dot_general(cg, hp.astype(BF16), _NT, preferred_element_type=F32)
            yscr_ref[:, sl] = ydiag + ystate * per_head(e_in, j)
            dxw = (dx * per_head(e_tail, j)).astype(BF16)
            upd = lax.dot_general(dxw, bg, _TN, preferred_element_type=F32)
            scale = None
            for hh in reversed(range(heads_per_pair)):
                h = j * heads_per_pair + hh
                col = e_all[:, h:h + 1]
                scale = col if scale is None else jnp.where(srow < (hh + 1) * P, col, scale)
            h_ref[j] = hp * scale + upd

    y = (yscr_ref[...] + dsk_ref[...] * xs) * _silu(z_ref[...])
    gw = D // G
    for g in range(G):
        sl = slice(g * gw, (g + 1) * gw)
        seg = y[:, sl]
        ms = jnp.mean(seg * seg, axis=-1, keepdims=True)
        y_ref[:, sl] = (seg * lax.rsqrt(ms + EPS) * nw_ref[:, sl]).astype(y_ref.dtype)

    @pl.when(c == nc - 1)
    def _():
        hout_ref[...] = h_ref[...]
        cout_ref[...] = last_rows


def _ssd(proj, dtr, conv0, h0, cw, cb, dtb, alog, dsk, nw, mixed, *, row0, nseq, T, L, D, G, P,
         name):
    M = proj.shape[0]
    nc = T // L
    blk0 = row0 // L
    npairs = D // LANES
    aliased = mixed is not None

    def rows(col):
        return pl.BlockSpec((L, D), lambda b, c: (blk0 + b * nc + c, col))

    def whole(shape):
        return pl.BlockSpec(shape, lambda b, c: (0,) * len(shape))

    in_specs = [
        rows(0), rows(1), rows(2),
        pl.BlockSpec((L, LANES), lambda b, c: (blk0 + b * nc + c, 0)),
        pl.BlockSpec((None, SUBLANES, 2 * D), lambda b, c: (b, 0, 0)),
        pl.BlockSpec((None, npairs, LANES, SSD_STATE), lambda b, c: (b, 0, 0, 0)),
        whole((SSD_CONV, D)), whole((SSD_CONV, D)), whole((1, D)), whole((1, D)),
        whole((1, LANES)), whole((1, LANES)), whole((1, D)), whole((1, D)),
        whole((LANES, LANES)),
    ]
    args = [proj, proj, proj, dtr, conv0, h0, cw[:, :D], cw[:, D:], cb[:, :D], cb[:, D:],
            dtb, alog, dsk, nw, jnp.eye(LANES, dtype=BF16)]
    if aliased:
        in_specs.append(pl.BlockSpec(memory_space=pl.ANY))
        args.append(mixed)
    return pl.pallas_call(
        functools.partial(_ssd_kernel, L=L, D=D, G=G, P=P, aliased=aliased),
        grid=(nseq, nc),
        in_specs=in_specs,
        out_specs=[
            pl.BlockSpec((L, D), lambda b, c: (blk0 + b * nc + c, 0)),
            pl.BlockSpec((None, npairs, LANES, SSD_STATE), lambda b, c: (b, 0, 0, 0)),
            pl.BlockSpec((None, SUBLANES, 2 * D), lambda b, c: (b, 0, 0)),
        ],
        out_shape=[
            jax.ShapeDtypeStruct((M, 2 * D), BF16),
            jax.ShapeDtypeStruct((nseq, npairs, LANES, SSD_STATE), F32),
            jax.ShapeDtypeStruct((nseq, SUBLANES, 2 * D), F32),
        ],
        scratch_shapes=[
            pltpu.VMEM((SUBLANES + L, 2 * D), F32),
            pltpu.VMEM((npairs, LANES, SSD_STATE), F32),
            pltpu.VMEM((L, D), F32),
        ],
        input_output_aliases={len(args) - 1: 0} if aliased else {},
        compiler_params=_params("parallel", "arbitrary"),
        name=name,
    )(*args)


def _hgrn_kernel(q_ref, f_ref, i_ref, g_ref, lb_ref, s0_ref, nw_ref, mixed_ref, o_ref, sout_ref,
                 st_ref, *, L, D):
    del mixed_ref
    c = pl.program_id(1)
    nc = pl.num_programs(1)
    H = D // HG_KEY

    @pl.when(c == 0)
    def _():
        st_ref[...] = s0_ref[...]

    lb = lb_ref[...]
    fz = f_ref[...]
    la = jnp.log(lb)
    lg = jnp.log1p(-lb) - (jnp.maximum(-fz, 0.0) + jnp.log(1.0 + jnp.exp(-jnp.abs(fz))))
    logf = jnp.maximum(la, lg) + jnp.log(1.0 + jnp.exp(-jnp.abs(la - lg)))
    q = q_ref[...]
    k = (1.0 - lb) * _sigmoid(-fz)
    v = i_ref[...]
    lf2 = logf * LOG2_E
    f = jnp.exp2(lf2)

    row = lax.broadcasted_iota(jnp.int32, (L, D), 0)
    r8 = row & (SUBLANES - 1)
    p = lf2
    t = lf2
    for s in (1, 2, 4):
        p = p + jnp.where(r8 >= s, _roll_in_block(p, s), 0.0)
        t = t + _roll_in_block(t, s)
    levels = []
    m = SUBLANES
    while m < L:
        second = (row & (2 * m - 1)) >= m
        ex = jnp.exp2(jnp.where(second, p, t - p))
        levels.append((m, (q * ex).astype(BF16), (k * ex).astype(BF16)))
        prev_t = _roll_rows(t, m)
        next_t = _roll_rows(t, L - m)
        p = p + jnp.where(second, prev_t, 0.0)
        t = t + jnp.where(second, prev_t, next_t)
        m *= 2
    q_in = (q * jnp.exp2(p)).astype(BF16)
    k_tail = (k * jnp.exp2(t - p)).astype(BF16)
    dec = jnp.exp2(t[0:1, :])
    vb = v.astype(BF16)

    rl = lax.broadcasted_iota(jnp.int32, (L, L), 0)
    cl = lax.broadcasted_iota(jnp.int32, (L, L), 1)
    lag_masks = [(cl == rl - d) & ((rl & (SUBLANES - 1)) >= d) for d in range(SUBLANES)]
    level_masks = [((rl >> ((2 * m).bit_length() - 1)) == (cl >> ((2 * m).bit_length() - 1)))
                   & ((rl & (2 * m - 1)) >= m) & ((cl & (2 * m - 1)) < m)
                   for (m, _, _) in levels]

    for h in range(H):
        sl = slice(h * HG_KEY, (h + 1) * HG_KEY)
        qh, fh = q[:, sl], f[:, sl]
        gd = k[:, sl]
        amat = jnp.where(lag_masks[0], jnp.sum(qh * gd, axis=-1, keepdims=True), 0.0)
        for d in range(1, SUBLANES):
            gd = fh * _roll_in_block(gd, 1)
            amat = jnp.where(lag_masks[d], jnp.sum(qh * gd, axis=-1, keepdims=True), amat)
        for (m, ql, kl), mask in zip(levels, level_masks):
            sc = lax.dot_general(ql[:, sl], kl[:, sl], _NT, preferred_element_type=F32)
            amat = jnp.where(mask, sc, amat)
        o = jnp.dot(amat.astype(BF16), vb[:, sl], preferred_element_type=F32)
        st = st_ref[h]
        o = o + lax.dot_general(q_in[:, sl], st.astype(BF16), _NT, preferred_element_type=F32)
        st_ref[h] = st * dec[:, sl] + lax.dot_general(vb[:, sl], k_tail[:, sl], _TN,
                                                      preferred_element_type=F32)
        ms = jnp.mean(o * o, axis=-1, keepdims=True)
        o = o * lax.rsqrt(ms + EPS)
        o_ref[:, sl] = (o * nw_ref[:, sl] * _silu(g_ref[:, sl])).astype(o_ref.dtype)

    @pl.when(c == nc - 1)
    def _():
        sout_ref[...] = st_ref[...]


def _hgrn(proj, lb, s0t, nw, mixed, *, row0, nseq, T, L, D, name):
    nc = T // L
    blk0 = row0 // L
    H = D // HG_KEY

    def rows(col):
        return pl.BlockSpec((L, D), lambda b, c: (blk0 + b * nc + c, col))

    return pl.pallas_call(
        functools.partial(_hgrn_kernel, L=L, D=D),
        grid=(nseq, nc),
        in_specs=[
            rows(3), rows(4), rows(5), rows(6),
            pl.BlockSpec((1, D), lambda b, c: (0, 0)),
            pl.BlockSpec((None, H, HG_KEY, HG_KEY), lambda b, c: (b, 0, 0, 0)),
            pl.BlockSpec((1, D), lambda b, c: (0, 0)),
            pl.BlockSpec(memory_space=pl.ANY),
        ],
        out_specs=[
            rows(1),
            pl.BlockSpec((None, H, HG_KEY, HG_KEY), lambda b, c: (b, 0, 0, 0)),
        ],
        out_shape=[
            jax.ShapeDtypeStruct(mixed.shape, mixed.dtype),
            jax.ShapeDtypeStruct((nseq, H, HG_KEY, HG_KEY), F32),
        ],
        scratch_shapes=[pltpu.VMEM((H, HG_KEY, HG_KEY), F32)],
        input_output_aliases={7: 0},
        compiler_params=_params("parallel", "arbitrary"),
        name=name,
    )(proj, proj, proj, proj, lb, s0t, nw, mixed)


def _pool_kernel(x_ref, buf_ref, nw0_ref, pw_ref, ps_ref, nw1_ref, xo_ref, tail_ref, ext_ref,
                 *, L, D, pos0):
    c = pl.program_id(1)
    hist = POOL_BUF + 1

    @pl.when(c == 0)
    def _():
        ext_ref[0:hist, :] = buf_ref[...]

    x = x_ref[...]
    ms = jnp.mean(x * x, axis=-1, keepdims=True)
    hn = x * lax.rsqrt(ms + EPS) * nw0_ref[...]
    ext_ref[hist:hist + L, :] = hn

    pos = pos0 + c * L + lax.broadcasted_iota(jnp.int32, (L, 1), 0)
    gw = D // len(POOL_WINDOWS)
    parts = []
    ss = None
    for gi, w in enumerate(POOL_WINDOWS):
        sl = slice(gi * gw, (gi + 1) * gw)
        s = ext_ref[:, sl]
        span = 1
        while span < w:
            s = s + _roll_rows(s, span)
            span *= 2
        cnt = jnp.minimum(pos + 1, w).astype(F32)
        pooled = s[hist:hist + L] / cnt - hn[:, sl]
        mixed = jnp.dot(pooled.astype(BF16), pw_ref[gi], preferred_element_type=F32) * ps_ref[:, sl]
        parts.append(mixed)
        sq = jnp.sum(mixed * mixed, axis=-1, keepdims=True)
        ss = sq if ss is None else ss + sq
    rs = lax.rsqrt(ss / D + EPS)
    for gi in range(len(POOL_WINDOWS)):
        sl = slice(gi * gw, (gi + 1) * gw)
        xo_ref[:, sl] = x[:, sl] + parts[gi] * rs * nw1_ref[:, sl]

    new_hist = ext_ref[L:L + hist, :]
    tail_ref[...] = new_hist
    ext_ref[0:hist, :] = new_hist


def _pool(x, buf, nw0, pw, ps, nw1, *, row0, nseq, T, L, D, pos0, name):
    nc = T // L
    blk0 = row0 // L
    hist = POOL_BUF + 1
    ng = len(POOL_WINDOWS)
    gw = D // ng
    vec = pl.BlockSpec((1, D), lambda b, c: (0, 0))
    xrows = pl.BlockSpec((L, D), lambda b, c: (blk0 + b * nc + c, 0))
    return pl.pallas_call(
        functools.partial(_pool_kernel, L=L, D=D, pos0=pos0),
        grid=(nseq, nc),
        in_specs=[
            xrows,
            pl.BlockSpec((None, hist, D), lambda b, c: (b, 0, 0)),
            vec,
            pl.BlockSpec((ng, gw, gw), lambda b, c: (0, 0, 0)),
            vec, vec,
        ],
        out_specs=[
            xrows,
            pl.BlockSpec((None, hist, D), lambda b, c: (b, 0, 0)),
        ],
        out_shape=[
            jax.ShapeDtypeStruct(x.shape, F32),
            jax.ShapeDtypeStruct((nseq, hist, D), F32),
        ],
        scratch_shapes=[pltpu.VMEM((hist + L, D), F32)],
        input_output_aliases={0: 0},
        compiler_params=_params("parallel", "arbitrary"),
        name=name,
    )(x, buf, nw0, pw, ps, nw1)


def _pad_lanes(v):
    return jnp.pad(v.astype(F32), (0, LANES - v.shape[0])).reshape(1, LANES)


def kernel(x_prompt, x_sample, state_conv, state_ssd, state_hgrn, state_pool, norm_w, w_in,
           conv_w, conv_b, dt_bias, a_log, d_skip, ssd_norm_w, hg_norm_w, hg_lower_bounds,
           w_out, pool_w, pool_scale, w_ffn_up, w_ffn_down):
    B, T, D = x_prompt.shape
    Bs, Ts, _ = x_sample.shape
    depth = norm_w.shape[0]
    heads = dt_bias.shape[1]
    P = D // heads
    conv_dim = conv_w.shape[-1]
    G = (conv_dim - D) // (2 * SSD_STATE)
    assert conv_dim == 2 * D and LANES % P == 0 and heads <= LANES and D % (G * LANES) == 0
    assert Ts >= POOL_BUF + 1 and T >= POOL_BUF + 1
    Mp, Ms = B * T, Bs * Ts
    npairs = D // LANES
    H = D // HG_KEY
    hist = POOL_BUF + 1
    keep = SSD_CONV - 1

    segs = (
        dict(row0=0, nseq=B, T=T),
        dict(row0=Mp, nseq=Bs, T=Ts),
    )

    x = jnp.concatenate([x_prompt.reshape(Mp, D), x_sample.reshape(Ms, D)], axis=0)

    lbs = jnp.cumsum(jax.nn.softmax(hg_lower_bounds.astype(F32), axis=0), axis=0)
    lbs = lbs - lbs[0]

    conv_out, ssd_out, hg_out, pool_out = ([], []), ([], []), ([], []), ([], [])
    for layer in range(depth):
        j = layer // 2
        nw = norm_w[layer].astype(F32)
        if layer % 2 == 0:
            wj = w_in[j]
            w_main = jnp.concatenate([wj[:, :3 * D], wj[:, 3 * D + heads:]], axis=1).astype(BF16)
            w_dt = jnp.pad(wj[:, 3 * D:3 * D + heads], ((0, 0), (0, LANES - heads))).astype(BF16)
            proj, dtr = _norm_mm(x, nw[0], w_main, w_dt, name=f"in_proj_{layer}")
            dsk = jnp.repeat(d_skip[j].astype(F32), P).reshape(1, D)
            mixed = None
            for si, seg in enumerate(segs):
                nseq, tt = seg["nseq"], seg["T"]
                if si == 0:
                    conv0 = jnp.zeros((nseq, SUBLANES, conv_dim), F32)
                    h0 = jnp.zeros((nseq, npairs, LANES, SSD_STATE), F32)
                else:
                    conv0 = jnp.pad(state_conv[j].astype(F32),
                                    ((0, 0), (SUBLANES - keep, 0), (0, 0)))
                    h0 = state_ssd[j].astype(F32).reshape(nseq, npairs, LANES, SSD_STATE)
                mixed, hfin, craw = _ssd(
                    proj, dtr, conv0, h0, conv_w[j].astype(F32),
                    conv_b[j].astype(F32).reshape(1, conv_dim), _pad_lanes(dt_bias[j]),
                    _pad_lanes(a_log[j]), dsk, ssd_norm_w[j].astype(F32).reshape(1, D), mixed,
                    L=_tile(tt, (128, 64, 32, 16)), D=D, G=G, P=P, name=f"ssd_{layer}_{si}", **seg)
                conv_out[si].append(craw[:, SUBLANES - keep:, :])
                ssd_out[si].append(hfin.reshape(nseq, heads, P, SSD_STATE))
            for si, seg in enumerate(segs):
                nseq, tt = seg["nseq"], seg["T"]
                if si == 0:
                    s0t = jnp.zeros((nseq, H, HG_KEY, HG_KEY), F32)
                else:
                    s0t = jnp.swapaxes(state_hgrn[j].astype(F32), -1, -2)
                mixed, sfin = _hgrn(proj, lbs[j].reshape(1, D), s0t,
                                    hg_norm_w[j].astype(F32).reshape(1, D), mixed,
                                    L=_tile(tt, (128, 64, 32, 16)), D=D, name=f"hgrn_{layer}_{si}", **seg)
                hg_out[si].append(jnp.swapaxes(sfin, -1, -2))
            x = _mm_res_norm(mixed, w_out[j].astype(BF16), x, nw[1], name=f"out_proj_{layer}")
        else:
            for si, seg in enumerate(segs):
                nseq, tt = seg["nseq"], seg["T"]
                if si == 0:
                    buf = jnp.zeros((nseq, hist, D), F32)
                else:
                    buf = jnp.pad(state_pool[j].astype(F32), ((0, 0), (1, 0), (0, 0)))
                x, tail = _pool(x, buf, nw[0].reshape(1, D), pool_w[j].astype(BF16),
                                pool_scale[j].astype(F32).reshape(1, D), nw[1].reshape(1, D),
                                L=_tile(tt, (256, 128, 64, 32, 16)), D=D,
                                pos0=0 if si == 0 else PAST_LEN, name=f"pool_{layer}_{si}", **seg)
                pool_out[si].append(tail[:, 1:, :])
        mlp = functools.partial(_ffn, x, nw[2], w_ffn_up[layer].astype(BF16),
                                w_ffn_down[layer].astype(BF16), nw[3])
        if layer < depth - 1:
            x = mlp(name=f"ffn_{layer}")
        else:
            y_p = mlp(row0=0, nrows=Mp, name=f"ffn_{layer}_0")
            y_s = mlp(row0=Mp, nrows=Ms, name=f"ffn_{layer}_1")

    dt_out = x_prompt.dtype
    y_prompt = y_p.reshape(B, T, D).astype(dt_out)
    y_sample = y_s.reshape(Bs, Ts, D).astype(dt_out)

    def stack(lst):
        return jnp.stack(lst).astype(dt_out)

    return (y_prompt, y_sample,
            stack(conv_out[0]), stack(ssd_out[0]), stack(hg_out[0]), stack(pool_out[0]),
            stack(conv_out[1]), stack(ssd_out[1]), stack(hg_out[1]), stack(pool_out[1]))
```

```python
import functools

import jax
import jax.numpy as jnp
from jax import lax
from jax.experimental import pallas as pl
from jax.experimental.pallas import tpu as pltpu

F32 = jnp.float32
BF16 = jnp.bfloat16
EPS = 1e-6
PAST_LEN = 4096
POOL_WINDOWS = (2, 4, 8, 16)
POOL_BUF = max(POOL_WINDOWS) - 1
SSD_STATE = 128
SSD_CONV = 4
HG_KEY = 128
LOG2_E = 1.4426950408889634
LANES = 128
SUBLANES = 8
VMEM_LIMIT_BYTES = 56 * 1024 * 1024

_NT = (((1,), (1,)), ((), ()))
_TN = (((0,), (0,)), ((), ()))


def _tile(n, candidates):
    for c in candidates:
        if n % c == 0:
            return c
    raise ValueError(f"no tile for {n} in {candidates}")


def _params(*sem):
    return pltpu.CompilerParams(dimension_semantics=sem, vmem_limit_bytes=VMEM_LIMIT_BYTES)


def _softplus(x):
    return jnp.maximum(x, 0.0) + jnp.log(1.0 + jnp.exp(-jnp.abs(x)))


def _sigmoid(x):
    return 0.5 * jnp.tanh(0.5 * x) + 0.5


def _silu(x):
    h = 0.5 * x
    return h * jnp.tanh(h) + h


def _roll_rows(x, shift):
    return pltpu.roll(x, shift, 0)


def _roll_in_block(x, shift):
    rows, width = x.shape
    x3 = x.reshape(rows // SUBLANES, SUBLANES, width)
    return pltpu.roll(x3, shift, 1).reshape(rows, width)


ROW_TILES = (1280, 1024, 512, 256, 192, 128, 64, 32, 16)
ROW_CHUNK = 640
ACC_COLS = 512


def _row_chunks(tm):
    rc = ROW_CHUNK if tm % ROW_CHUNK == 0 else tm
    return [slice(r0, r0 + rc) for r0 in range(0, tm, rc)]


def _rms_rows(x, w):
    ms = jnp.mean(x * x, axis=-1, keepdims=True)
    return x * lax.rsqrt(ms + EPS) * w


def _accumulate(acc_ref, rows, lhs, w_ref):
    n = acc_ref.shape[1]
    step = min(ACC_COLS, n)
    for n0 in range(0, n, step):
        cols = slice(n0, n0 + step)
        acc_ref[rows, cols] += jnp.dot(lhs, w_ref[:, cols], preferred_element_type=F32)


def _norm_mm_kernel(x_ref, nw_ref, w_ref, w2_ref, o_ref, o2_ref, hn_ref):
    chunks = _row_chunks(x_ref.shape[0])

    @pl.when(pl.program_id(1) == 0)
    def _():
        for rows in chunks:
            hn_ref[rows, :] = _rms_rows(x_ref[rows, :], nw_ref[...]).astype(BF16)
            o2_ref[rows, :] = jnp.dot(hn_ref[rows, :], w2_ref[...], preferred_element_type=F32)

    for rows in chunks:
        o_ref[rows, :] = jnp.dot(hn_ref[rows, :], w_ref[...],
                                 preferred_element_type=F32).astype(o_ref.dtype)


def _norm_mm(x, nw, w, w2, *, name):
    M, K = x.shape
    N = w.shape[1]
    tm = _tile(M, ROW_TILES)
    tn = _tile(N, (1024, 512, 256, 128))
    return pl.pallas_call(
        _norm_mm_kernel,
        grid=(M // tm, N // tn),
        in_specs=[
            pl.BlockSpec((tm, K), lambda i, j: (i, 0)),
            pl.BlockSpec((1, K), lambda i, j: (0, 0)),
            pl.BlockSpec((K, tn), lambda i, j: (0, j)),
            pl.BlockSpec((K, LANES), lambda i, j: (0, 0)),
        ],
        out_specs=[
            pl.BlockSpec((tm, tn), lambda i, j: (i, j)),
            pl.BlockSpec((tm, LANES), lambda i, j: (i, 0)),
        ],
        out_shape=[
            jax.ShapeDtypeStruct((M, N), F32),
            jax.ShapeDtypeStruct((M, LANES), F32),
        ],
        scratch_shapes=[pltpu.VMEM((tm, K), BF16)],
        compiler_params=_params("parallel", "arbitrary"),
        name=name,
    )(x, nw.reshape(1, K), w, w2)


def _mm_res_norm_kernel(x_ref, w_ref, r_ref, nw_ref, o_ref, *, nk):
    k = pl.program_id(1)
    chunks = _row_chunks(x_ref.shape[0])

    @pl.when(k == 0)
    def _():
        o_ref[...] = jnp.zeros_like(o_ref)

    for rows in chunks:
        _accumulate(o_ref, rows, x_ref[rows, :], w_ref)

    @pl.when(k == nk - 1)
    def _():
        for rows in chunks:
            o_ref[rows, :] = r_ref[rows, :] + _rms_rows(o_ref[rows, :], nw_ref[...])


def _mm_res_norm(x, w_all, layer, resid, nw, *, name):
    M, K = x.shape
    N = w_all.shape[2]
    tm = _tile(M, ROW_TILES)
    tk = _tile(K, (1024, 512, 256, 128))
    nk = K // tk
    return pl.pallas_call(
        functools.partial(_mm_res_norm_kernel, nk=nk),
        grid=(M // tm, nk),
        in_specs=[
            pl.BlockSpec((tm, tk), lambda i, k: (i, k)),
            pl.BlockSpec((None, tk, N), lambda i, k: (layer, k, 0)),
            pl.BlockSpec((tm, N), lambda i, k: (i, 0)),
            pl.BlockSpec((1, N), lambda i, k: (0, 0)),
        ],
        out_specs=pl.BlockSpec((tm, N), lambda i, k: (i, 0), pipeline_mode=pl.Buffered(1)),
        out_shape=jax.ShapeDtypeStruct((M, N), F32),
        compiler_params=_params("parallel", "arbitrary"),
        name=name,
    )(x, w_all, resid, nw.reshape(1, N))


def _ffn_kernel(x_ref, nw_in_ref, wu_ref, wd_ref, nw_out_ref, o_ref, hn_ref, *, nf):
    f = pl.program_id(1)
    chunks = _row_chunks(x_ref.shape[0])

    @pl.when(f == 0)
    def _():
        o_ref[...] = jnp.zeros_like(o_ref)
        for rows in chunks:
            hn_ref[rows, :] = _rms_rows(x_ref[rows, :], nw_in_ref[...]).astype(BF16)

    for rows in chunks:
        h = jnp.dot(hn_ref[rows, :], wu_ref[...], preferred_element_type=F32)
        h = jnp.square(jnp.maximum(h, 0.0)).astype(BF16)
        _accumulate(o_ref, rows, h, wd_ref)

    @pl.when(f == nf - 1)
    def _():
        for rows in chunks:
            o_ref[rows, :] = x_ref[rows, :] + _rms_rows(o_ref[rows, :], nw_out_ref[...])


def _ffn(x, nw_in, wu_all, wd_all, layer, nw_out, *, row0=0, nrows=None, name):
    D = x.shape[1]
    M = x.shape[0] if nrows is None else nrows
    F = wu_all.shape[2]
    tm = _tile(M, tuple(c for c in ROW_TILES if row0 % c == 0))
    blk0 = row0 // tm
    tf = _tile(F, (512, 256, 128))
    nf = F // tf
    vec = pl.BlockSpec((1, D), lambda i, f: (0, 0))
    return pl.pallas_call(
        functools.partial(_ffn_kernel, nf=nf),
        grid=(M // tm, nf),
        in_specs=[
            pl.BlockSpec((tm, D), lambda i, f: (blk0 + i, 0)),
            vec,
            pl.BlockSpec((None, D, tf), lambda i, f: (layer, 0, f)),
            pl.BlockSpec((None, tf, D), lambda i, f: (layer, f, 0)),
            vec,
        ],
        out_specs=pl.BlockSpec((tm, D), lambda i, f: (i, 0), pipeline_mode=pl.Buffered(1)),
        out_shape=jax.ShapeDtypeStruct((M, D), F32),
        scratch_shapes=[pltpu.VMEM((tm, D), BF16)],
        compiler_params=_params("parallel", "arbitrary"),
        name=name,
    )(x, nw_in.reshape(1, D), wu_all, wd_all, nw_out.reshape(1, D))


def _transpose_exact(a, eye):
    out = None
    r = a
    for _ in range(3):
        p = r.astype(BF16)
        r = r - p.astype(F32)
        t = lax.dot_general(eye, p, _NT, preferred_element_type=F32)
        out = t if out is None else out + t
    return out


def _ssd_kernel(*refs, L, D, G, P, zero_init, n_alias):
    z_ref, xr_ref, bcr_ref, dtr_ref = refs[:4]
    n_state = 0 if zero_init else 2
    (cwx_ref, cwbc_ref, cbx_ref, cbbc_ref, dtb_ref, alog_ref, dsk_ref, nw_ref,
     eye_ref) = refs[4 + n_state:13 + n_state]
    y_ref, hout_ref, cout_ref, ext_ref, h_ref, yscr_ref = refs[13 + n_state + n_alias:]
    c = pl.program_id(1)
    nc = pl.num_programs(1)
    N = SSD_STATE
    npairs = D // LANES
    pairs_per_group = npairs // G
    heads_per_pair = LANES // P

    @pl.when(c == 0)
    def _():
        if zero_init:
            ext_ref[0:SUBLANES, :] = jnp.zeros((SUBLANES, 2 * D), F32)
            h_ref[...] = jnp.zeros_like(h_ref)
        else:
            ext_ref[0:SUBLANES, :] = refs[4][...]
            h_ref[...] = refs[5][...]

    ext_ref[SUBLANES:SUBLANES + L, 0:D] = xr_ref[...]
    ext_ref[SUBLANES:SUBLANES + L, D:2 * D] = bcr_ref[...]

    rsub = lax.broadcasted_iota(jnp.int32, (L, D), 0) & (SUBLANES - 1)

    def conv(lo, w_ref, b_ref):
        u = ext_ref[:, lo:lo + D]
        acc = u[SUBLANES:] * w_ref[SSD_CONV - 1:SSD_CONV, :]
        for s in range(1, SSD_CONV):
            r = _roll_in_block(u, s)
            shifted = jnp.where(rsub >= s, r[SUBLANES:], r[:L])
            acc = acc + shifted * w_ref[SSD_CONV - 1 - s:SSD_CONV - s, :]
        return _silu(acc + b_ref[...])

    xs = conv(0, cwx_ref, cbx_ref)
    bc = conv(D, cwbc_ref, cbbc_ref)
    last_rows = ext_ref[L:L + SUBLANES, :]
    ext_ref[0:SUBLANES, :] = last_rows

    dt = _softplus(dtr_ref[...] + dtb_ref[...])
    a = dt * (-LOG2_E * jnp.exp(alog_ref[...]))
    row = lax.broadcasted_iota(jnp.int32, (L, LANES), 0)
    acs = a
    s = 1
    while s < L:
        acs = acs + jnp.where(row >= s, _roll_rows(acs, s), 0.0)
        s *= 2
    acs_t = _transpose_exact(acs, eye_ref[...])
    last = acs[L - 1:L, :]
    e_in = jnp.exp2(acs)
    e_tail = jnp.exp2(last - acs)
    e_all = jnp.exp2(last)

    lane = lax.broadcasted_iota(jnp.int32, (L, LANES), 1)
    tri = (lax.broadcasted_iota(jnp.int32, (L, L), 0)
           >= lax.broadcasted_iota(jnp.int32, (L, L), 1))
    srow = lax.broadcasted_iota(jnp.int32, (LANES, LANES), 0)

    def per_head(arr, j, rows=lane):
        out = None
        for hh in reversed(range(heads_per_pair)):
            h = j * heads_per_pair + hh
            col = arr[:, h:h + 1]
            out = col if out is None else jnp.where(rows < (hh + 1) * P, col, out)
        return out

    for g in range(G):
        bg = bc[:, g * N:(g + 1) * N].astype(BF16)
        cg = bc[:, (G + g) * N:(G + g + 1) * N].astype(BF16)
        cb = lax.dot_general(cg, bg, _NT, preferred_element_type=F32)
        cb = jnp.where(tri, cb, 0.0)
        for jj in range(pairs_per_group):
            j = g * pairs_per_group + jj
            sl = slice(j * LANES, (j + 1) * LANES)
            dx = per_head(dt, j) * xs[:, sl]
            dxb = dx.astype(BF16)
            ydiag = None
            for hh in reversed(range(heads_per_pair)):
                h = j * heads_per_pair + hh
                seg = jnp.minimum(acs[:, h:h + 1] - acs_t[h:h + 1, :], 0.0)
                m = (cb * jnp.exp2(seg)).astype(BF16)
                yh = jnp.dot(m, dxb, preferred_element_type=F32)
                ydiag = yh if ydiag is None else jnp.where(lane < (hh + 1) * P, yh, ydiag)
            hp = h_ref[j]
            ystate = lax.dot_general(cg, hp.astype(BF16), _NT, preferred_element_type=F32)
            yscr_ref[:, sl] = ydiag + ystate * per_head(e_in, j)
            dxw = (dx * per_head(e_tail, j)).astype(BF16)
            upd = lax.dot_general(dxw, bg, _TN, preferred_element_type=F32)
            scale = None
            for hh in reversed(range(heads_per_pair)):
                h = j * heads_per_pair + hh
                col = e_all[:, h:h + 1]
                scale = col if scale is None else jnp.where(srow < (hh + 1) * P, col, scale)
            h_ref[j] = hp * scale + upd

    y = (yscr_ref[...] + dsk_ref[...] * xs) * _silu(z_ref[...])
    gw = D // G
    for g in range(G):
        sl = slice(g * gw, (g + 1) * gw)
        seg = y[:, sl]
        ms = jnp.mean(seg * seg, axis=-1, keepdims=True)
        y_ref[:, sl] = (seg * lax.rsqrt(ms + EPS) * nw_ref[:, sl]).astype(y_ref.dtype)

    @pl.when(c == nc - 1)
    def _():
        hout_ref[...] = h_ref[...]
        cout_ref[...] = last_rows


def _ssd(proj, dtr, state, cw, cb, dtb, alog, dsk, nw, prev, *, layer, n_layers, row0, nseq, T, L,
         D, G, P, name):
    M = proj.shape[0]
    nc = T // L
    blk0 = row0 // L
    npairs = D // LANES
    zero_init = state is None

    def rows(col):
        return pl.BlockSpec((L, D), lambda b, c: (blk0 + b * nc + c, col))

    def whole(shape):
        return pl.BlockSpec(shape, lambda b, c: (0,) * len(shape))

    h_spec = pl.BlockSpec((None, None, npairs, LANES, SSD_STATE), lambda b, c: (layer, b, 0, 0, 0))
    c_spec = pl.BlockSpec((None, None, SUBLANES, 2 * D), lambda b, c: (layer, b, 0, 0))
    in_specs = [rows(0), rows(1), rows(2),
                pl.BlockSpec((L, LANES), lambda b, c: (blk0 + b * nc + c, 0))]
    args = [proj, proj, proj, dtr]
    if not zero_init:
        in_specs += [c_spec, h_spec]
        args += list(state)
    in_specs += [whole((SSD_CONV, D)), whole((SSD_CONV, D)), whole((1, D)), whole((1, D)),
                 whole((1, LANES)), whole((1, LANES)), whole((1, D)), whole((1, D)),
                 whole((LANES, LANES))]
    args += [cw[:, :D], cw[:, D:], cb[:, :D], cb[:, D:], dtb, alog, dsk, nw,
             jnp.eye(LANES, dtype=BF16)]
    aliases = {}
    for k, buf in enumerate(prev):
        if buf is not None:
            aliases[len(args)] = k
            in_specs.append(pl.BlockSpec(memory_space=pl.ANY))
            args.append(buf)
    return pl.pallas_call(
        functools.partial(_ssd_kernel, L=L, D=D, G=G, P=P, zero_init=zero_init,
                          n_alias=len(aliases)),
        grid=(nseq, nc),
        in_specs=in_specs,
        out_specs=[pl.BlockSpec((L, D), lambda b, c: (blk0 + b * nc + c, 0)), h_spec, c_spec],
        out_shape=[
            jax.ShapeDtypeStruct((M, 2 * D), BF16),
            jax.ShapeDtypeStruct((n_layers, nseq, npairs, LANES, SSD_STATE), F32),
            jax.ShapeDtypeStruct((n_layers, nseq, SUBLANES, 2 * D), F32),
        ],
        scratch_shapes=[
            pltpu.VMEM((SUBLANES + L, 2 * D), F32),
            pltpu.VMEM((npairs, LANES, SSD_STATE), F32),
            pltpu.VMEM((L, D), F32),
        ],
        input_output_aliases=aliases,
        compiler_params=_params("parallel", "arbitrary"),
        name=name,
    )(*args)


def _hgrn_kernel(*refs, L, D, zero_init, n_alias):
    q_ref, f_ref, i_ref, g_ref, lb_ref, nw_ref = refs[:6]
    o_ref, sout_ref, st_ref = refs[(6 if zero_init else 7) + n_alias:]
    c = pl.program_id(1)
    nc = pl.num_programs(1)
    H = D // HG_KEY

    @pl.when(c == 0)
    def _():
        if zero_init:
            st_ref[...] = jnp.zeros_like(st_ref)
        else:
            for h in range(H):
                st_ref[h] = refs[6][h].T

    lb = lb_ref[...]
    fz = f_ref[...]
    la = jnp.log(lb)
    lg = jnp.log1p(-lb) - (jnp.maximum(-fz, 0.0) + jnp.log(1.0 + jnp.exp(-jnp.abs(fz))))
    logf = jnp.maximum(la, lg) + jnp.log(1.0 + jnp.exp(-jnp.abs(la - lg)))
    q = q_ref[...]
    k = (1.0 - lb) * _sigmoid(-fz)
    v = i_ref[...]
    lf2 = logf * LOG2_E
    f = jnp.exp2(lf2)

    row = lax.broadcasted_iota(jnp.int32, (L, D), 0)
    r8 = row & (SUBLANES - 1)
    p = lf2
    t = lf2
    for s in (1, 2, 4):
        p = p + jnp.where(r8 >= s, _roll_in_block(p, s), 0.0)
        t = t + _roll_in_block(t, s)
    levels = []
    m = SUBLANES
    while m < L:
        second = (row & (2 * m - 1)) >= m
        ex = jnp.exp2(jnp.where(second, p, t - p))
        levels.append((m, (q * ex).astype(BF16), (k * ex).astype(BF16)))
        prev_t = _roll_rows(t, m)
        next_t = _roll_rows(t, L - m)
        p = p + jnp.where(second, prev_t, 0.0)
        t = t + jnp.where(second, prev_t, next_t)
        m *= 2
    q_in = (q * jnp.exp2(p)).astype(BF16)
    k_tail = (k * jnp.exp2(t - p)).astype(BF16)
    dec = jnp.exp2(t[0:1, :])
    vb = v.astype(BF16)

    rl = lax.broadcasted_iota(jnp.int32, (L, L), 0)
    cl = lax.broadcasted_iota(jnp.int32, (L, L), 1)
    lag_masks = [(cl == rl - d) & ((rl & (SUBLANES - 1)) >= d) for d in range(SUBLANES)]
    level_masks = [((rl >> ((2 * m).bit_length() - 1)) == (cl >> ((2 * m).bit_length() - 1)))
                   & ((rl & (2 * m - 1)) >= m) & ((cl & (2 * m - 1)) < m)
                   for (m, _, _) in levels]

    for h in range(H):
        sl = slice(h * HG_KEY, (h + 1) * HG_KEY)
        qh, fh = q[:, sl], f[:, sl]
        gd = k[:, sl]
        amat = jnp.where(lag_masks[0], jnp.sum(qh * gd, axis=-1, keepdims=True), 0.0)
        for d in range(1, SUBLANES):
            gd = fh * _roll_in_block(gd, 1)
            amat = jnp.where(lag_masks[d], jnp.sum(qh * gd, axis=-1, keepdims=True), amat)
        for (m, ql, kl), mask in zip(levels, level_masks):
            sc = lax.dot_general(ql[:, sl], kl[:, sl], _NT, preferred_element_type=F32)
            amat = jnp.where(mask, sc, amat)
        o = jnp.dot(amat.astype(BF16), vb[:, sl], preferred_element_type=F32)
        st = st_ref[h]
        o = o + lax.dot_general(q_in[:, sl], st.astype(BF16), _NT, preferred_element_type=F32)
        st_ref[h] = st * dec[:, sl] + lax.dot_general(vb[:, sl], k_tail[:, sl], _TN,
                                                      preferred_element_type=F32)
        ms = jnp.mean(o * o, axis=-1, keepdims=True)
        o = o * lax.rsqrt(ms + EPS)
        o_ref[:, sl] = (o * nw_ref[:, sl] * _silu(g_ref[:, sl])).astype(o_ref.dtype)

    @pl.when(c == nc - 1)
    def _():
        for h in range(H):
            sout_ref[h] = st_ref[h].T


def _hgrn(proj, lb, state, nw, prev, *, layer, n_layers, row0, nseq, T, L, D, name):
    nc = T // L
    blk0 = row0 // L
    H = D // HG_KEY
    zero_init = state is None

    def rows(col):
        return pl.BlockSpec((L, D), lambda b, c: (blk0 + b * nc + c, col))

    vec = pl.BlockSpec((1, D), lambda b, c: (0, 0))
    s_spec = pl.BlockSpec((None, None, H, HG_KEY, HG_KEY), lambda b, c: (layer, b, 0, 0, 0))
    in_specs = [rows(3), rows(4), rows(5), rows(6), vec, vec]
    args = [proj, proj, proj, proj, lb, nw]
    if not zero_init:
        in_specs.append(s_spec)
        args.append(state)
    aliases = {}
    for k, buf in enumerate(prev):
        if buf is not None:
            aliases[len(args)] = k
            in_specs.append(pl.BlockSpec(memory_space=pl.ANY))
            args.append(buf)
    return pl.pallas_call(
        functools.partial(_hgrn_kernel, L=L, D=D, zero_init=zero_init, n_alias=len(aliases)),
        grid=(nseq, nc),
        in_specs=in_specs,
        out_specs=[rows(1), s_spec],
        out_shape=[
            jax.ShapeDtypeStruct(prev[0].shape, prev[0].dtype),
            jax.ShapeDtypeStruct((n_layers, nseq, H, HG_KEY, HG_KEY), F32),
        ],
        scratch_shapes=[pltpu.VMEM((H, HG_KEY, HG_KEY), F32)],
        input_output_aliases=aliases,
        compiler_params=_params("parallel", "arbitrary"),
        name=name,
    )(*args)


def _pool_kernel(*refs, L, D, pos0, zero_init, n_alias):
    x_ref, nw0_ref, pw_ref, ps_ref, nw1_ref = refs[:5]
    xo_ref, tail_ref, ext_ref = refs[(5 if zero_init else 6) + n_alias:]
    c = pl.program_id(1)
    hist = POOL_BUF + 1

    @pl.when(c == 0)
    def _():
        if zero_init:
            ext_ref[0:hist, :] = jnp.zeros((hist, D), F32)
        else:
            ext_ref[0:hist, :] = refs[5][...]

    x = x_ref[...]
    ms = jnp.mean(x * x, axis=-1, keepdims=True)
    hn = x * lax.rsqrt(ms + EPS) * nw0_ref[...]
    ext_ref[hist:hist + L, :] = hn

    pos = pos0 + c * L + lax.broadcasted_iota(jnp.int32, (L, 1), 0)
    gw = D // len(POOL_WINDOWS)
    parts = []
    ss = None
    for gi, w in enumerate(POOL_WINDOWS):
        sl = slice(gi * gw, (gi + 1) * gw)
        s = ext_ref[:, sl]
        span = 1
        while span < w:
            s = s + _roll_rows(s, span)
            span *= 2
        cnt = jnp.minimum(pos + 1, w).astype(F32)
        pooled = s[hist:hist + L] / cnt - hn[:, sl]
        mixed = jnp.dot(pooled.astype(BF16), pw_ref[gi], preferred_element_type=F32) * ps_ref[:, sl]
        parts.append(mixed)
        sq = jnp.sum(mixed * mixed, axis=-1, keepdims=True)
        ss = sq if ss is None else ss + sq
    rs = lax.rsqrt(ss / D + EPS)
    for gi in range(len(POOL_WINDOWS)):
        sl = slice(gi * gw, (gi + 1) * gw)
        xo_ref[:, sl] = x[:, sl] + parts[gi] * rs * nw1_ref[:, sl]

    new_hist = ext_ref[L:L + hist, :]
    tail_ref[...] = new_hist
    ext_ref[0:hist, :] = new_hist


def _pool(x, buf, nw0, pw_all, ps, nw1, prev_tail, *, layer, n_layers, row0, nseq, T, L, D, pos0,
          name):
    nc = T // L
    blk0 = row0 // L
    hist = POOL_BUF + 1
    ng = len(POOL_WINDOWS)
    gw = D // ng
    zero_init = buf is None
    vec = pl.BlockSpec((1, D), lambda b, c: (0, 0))
    xrows = pl.BlockSpec((L, D), lambda b, c: (blk0 + b * nc + c, 0))
    t_spec = pl.BlockSpec((None, None, hist, D), lambda b, c: (layer, b, 0, 0))
    in_specs = [xrows, vec,
                pl.BlockSpec((None, ng, gw, gw), lambda b, c: (layer, 0, 0, 0)), vec, vec]
    args = [x, nw0, pw_all, ps, nw1]
    if not zero_init:
        in_specs.append(t_spec)
        args.append(buf)
    aliases = {0: 0}
    if prev_tail is not None:
        aliases[len(args)] = 1
        in_specs.append(pl.BlockSpec(memory_space=pl.ANY))
        args.append(prev_tail)
    return pl.pallas_call(
        functools.partial(_pool_kernel, L=L, D=D, pos0=pos0, zero_init=zero_init,
                          n_alias=len(aliases) - 1),
        grid=(nseq, nc),
        in_specs=in_specs,
        out_specs=[xrows, t_spec],
        out_shape=[
            jax.ShapeDtypeStruct(x.shape, F32),
            jax.ShapeDtypeStruct((n_layers, nseq, hist, D), F32),
        ],
        scratch_shapes=[pltpu.VMEM((hist + L, D), F32)],
        input_output_aliases=aliases,
        compiler_params=_params("parallel", "arbitrary"),
        name=name,
    )(*args)


def _pad_lanes(v):
    return jnp.pad(v.astype(F32), (0, LANES - v.shape[0])).reshape(1, LANES)


def kernel(x_prompt, x_sample, state_conv, state_ssd, state_hgrn, state_pool, norm_w, w_in,
           conv_w, conv_b, dt_bias, a_log, d_skip, ssd_norm_w, hg_norm_w, hg_lower_bounds,
           w_out, pool_w, pool_scale, w_ffn_up, w_ffn_down):
    B, T, D = x_prompt.shape
    Bs, Ts, _ = x_sample.shape
    depth = norm_w.shape[0]
    n_ab, n_c = w_in.shape[0], pool_w.shape[0]
    heads = dt_bias.shape[1]
    P = D // heads
    conv_dim = conv_w.shape[-1]
    G = (conv_dim - D) // (2 * SSD_STATE)
    assert conv_dim == 2 * D and LANES % P == 0 and heads <= LANES and D % (G * LANES) == 0
    assert Ts >= POOL_BUF + 1 and T >= POOL_BUF + 1
    Mp, Ms = B * T, Bs * Ts
    npairs = D // LANES
    keep = SSD_CONV - 1

    segs = (dict(row0=0, nseq=B, T=T), dict(row0=Mp, nseq=Bs, T=Ts))
    ssd_in = (None, (jnp.pad(state_conv.astype(F32), ((0, 0), (0, 0), (SUBLANES - keep, 0), (0, 0))),
                     state_ssd.astype(F32).reshape(n_ab, Bs, npairs, LANES, SSD_STATE)))
    hg_in = (None, state_hgrn.astype(F32))
    pool_in = (None, jnp.pad(state_pool.astype(F32), ((0, 0), (0, 0), (1, 0), (0, 0))))
    pos0 = (0, PAST_LEN)

    x = jnp.concatenate([x_prompt.reshape(Mp, D), x_sample.reshape(Ms, D)], axis=0)

    lbs = jnp.cumsum(jax.nn.softmax(hg_lower_bounds.astype(F32), axis=0), axis=0)
    lbs = lbs - lbs[0]

    w_out_b = w_out.astype(BF16)
    w_up_b = w_ffn_up.astype(BF16)
    w_down_b = w_ffn_down.astype(BF16)
    pool_w_b = pool_w.astype(BF16)

    ssd_st, conv_st, hg_st, pool_st = [None, None], [None, None], [None, None], [None, None]
    for layer in range(depth):
        j = layer // 2
        nw = norm_w[layer].astype(F32)
        if layer % 2 == 0:
            wj = w_in[j]
            w_main = jnp.concatenate([wj[:, :3 * D], wj[:, 3 * D + heads:]], axis=1).astype(BF16)
            w_dt = jnp.pad(wj[:, 3 * D:3 * D + heads], ((0, 0), (0, LANES - heads))).astype(BF16)
            proj, dtr = _norm_mm(x, nw[0], w_main, w_dt, name=f"in_proj_{layer}")
            dsk = jnp.repeat(d_skip[j].astype(F32), P).reshape(1, D)
            mixed = None
            for si, seg in enumerate(segs):
                mixed, ssd_st[si], conv_st[si] = _ssd(
                    proj, dtr, ssd_in[si], conv_w[j].astype(F32),
                    conv_b[j].astype(F32).reshape(1, conv_dim), _pad_lanes(dt_bias[j]),
                    _pad_lanes(a_log[j]), dsk, ssd_norm_w[j].astype(F32).reshape(1, D),
                    (mixed, ssd_st[si], conv_st[si]), layer=j, n_layers=n_ab,
                    L=_tile(seg["T"], (128, 64, 32, 16)), D=D, G=G, P=P,
                    name=f"ssd_{layer}_{si}", **seg)
            for si, seg in enumerate(segs):
                mixed, hg_st[si] = _hgrn(
                    proj, lbs[j].reshape(1, D), hg_in[si], hg_norm_w[j].astype(F32).reshape(1, D),
                    (mixed, hg_st[si]), layer=j, n_layers=n_ab,
                    L=_tile(seg["T"], (128, 64, 32, 16)), D=D, name=f"hgrn_{layer}_{si}", **seg)
            x = _mm_res_norm(mixed, w_out_b, j, x, nw[1], name=f"out_proj_{layer}")
        else:
            for si, seg in enumerate(segs):
                x, pool_st[si] = _pool(
                    x, pool_in[si], nw[0].reshape(1, D), pool_w_b,
                    pool_scale[j].astype(F32).reshape(1, D), nw[1].reshape(1, D), pool_st[si],
                    layer=j, n_layers=n_c, L=_tile(seg["T"], (256, 128, 64, 32, 16)), D=D,
                    pos0=pos0[si], name=f"pool_{layer}_{si}", **seg)
        mlp = functools.partial(_ffn, x, nw[2], w_up_b, w_down_b, layer, nw[3])
        if layer < depth - 1:
            x = mlp(name=f"ffn_{layer}")
        else:
            y_p = mlp(row0=0, nrows=Mp, name=f"ffn_{layer}_0")
            y_s = mlp(row0=Mp, nrows=Ms, name=f"ffn_{layer}_1")

    dt_out = x_prompt.dtype
    nseqs = (B, Bs)
    outs = [y_p.reshape(B, T, D), y_s.reshape(Bs, Ts, D)]
    for si in range(2):
        outs += [conv_st[si][:, :, SUBLANES - keep:, :],
                 ssd_st[si].reshape(n_ab, nseqs[si], heads, P, SSD_STATE),
                 hg_st[si],
                 pool_st[si][:, :, 1:, :]]
    return tuple(o.astype(dt_out) for o in outs)
```

```python
import functools

import jax
import jax.numpy as jnp
from jax import lax
from jax.experimental import pallas as pl
from jax.experimental.pallas import tpu as pltpu

F32 = jnp.float32
BF16 = jnp.bfloat16
EPS = 1e-6
PAST_LEN = 4096
POOL_WINDOWS = (2, 4, 8, 16)
POOL_BUF = max(POOL_WINDOWS) - 1
SSD_STATE = 128
SSD_CONV = 4
HG_KEY = 128
LOG2_E = 1.4426950408889634
LANES = 128
SUBLANES = 8
VMEM_LIMIT_BYTES = 56 * 1024 * 1024

_NT = (((1,), (1,)), ((), ()))
_TN = (((0,), (0,)), ((), ()))


def _tile(n, candidates):
    for c in candidates:
        if n % c == 0:
            return c
    raise ValueError(f"no tile for {n} in {candidates}")


def _params(*sem):
    return pltpu.CompilerParams(dimension_semantics=sem, vmem_limit_bytes=VMEM_LIMIT_BYTES)


def _softplus(x):
    return jnp.maximum(x, 0.0) + jnp.log(1.0 + jnp.exp(-jnp.abs(x)))


def _sigmoid(x):
    return 0.5 * jnp.tanh(0.5 * x) + 0.5


def _silu(x):
    h = 0.5 * x
    return h * jnp.tanh(h) + h


def _roll_rows(x, shift):
    return pltpu.roll(x, shift, 0)


def _roll_in_block(x, shift):
    rows, width = x.shape
    x3 = x.reshape(rows // SUBLANES, SUBLANES, width)
    return pltpu.roll(x3, shift, 1).reshape(rows, width)


ROW_TILES = (1280, 1024, 512, 256, 192, 128, 64, 32, 16)
ROW_CHUNK = 640
ACC_COLS = 512


def _row_chunks(tm):
    rc = ROW_CHUNK if tm % ROW_CHUNK == 0 else tm
    return [slice(r0, r0 + rc) for r0 in range(0, tm, rc)]


def _rms_rows(x, w):
    ms = jnp.mean(x * x, axis=-1, keepdims=True)
    return x * lax.rsqrt(ms + EPS) * w


def _accumulate(acc_ref, rows, lhs, w_ref, first):
    n = acc_ref.shape[1]
    step = min(ACC_COLS, n)
    for n0 in range(0, n, step):
        cols = slice(n0, n0 + step)
        d = jnp.dot(lhs, w_ref[:, cols], preferred_element_type=F32)
        if first:
            acc_ref[rows, cols] = d
        else:
            acc_ref[rows, cols] += d


def _by_position(step, nsteps, body):
    if nsteps == 1:
        body(True, True)
        return
    pl.when(step == 0)(lambda: body(True, False))
    if nsteps > 2:
        pl.when(jnp.logical_and(step > 0, step < nsteps - 1))(lambda: body(False, False))
    pl.when(step == nsteps - 1)(lambda: body(False, True))


def _norm_mm_kernel(x_ref, nw_ref, w_ref, w2_ref, o_ref, o2_ref, hn_ref, *, nj):
    chunks = _row_chunks(x_ref.shape[0])

    def body(first, last):
        del last
        for rows in chunks:
            if first:
                hn_ref[rows, :] = _rms_rows(x_ref[rows, :], nw_ref[...]).astype(BF16)
                o2_ref[rows, :] = jnp.dot(hn_ref[rows, :], w2_ref[...],
                                          preferred_element_type=F32)
            o_ref[rows, :] = jnp.dot(hn_ref[rows, :], w_ref[...],
                                     preferred_element_type=F32).astype(o_ref.dtype)

    j = pl.program_id(1)
    pl.when(j == 0)(lambda: body(True, False))
    if nj > 1:
        pl.when(j > 0)(lambda: body(False, False))


def _norm_mm(x, nw, w_all, w2_all, layer, *, name):
    M, K = x.shape
    N = w_all.shape[2]
    tm = _tile(M, ROW_TILES)
    tn = _tile(N, (1024, 512, 256, 128))
    return pl.pallas_call(
        functools.partial(_norm_mm_kernel, nj=N // tn),
        grid=(M // tm, N // tn),
        in_specs=[
            pl.BlockSpec((tm, K), lambda i, j: (i, 0)),
            pl.BlockSpec((1, K), lambda i, j: (0, 0)),
            pl.BlockSpec((None, K, tn), lambda i, j: (layer, 0, j)),
            pl.BlockSpec((None, K, LANES), lambda i, j: (layer, 0, 0)),
        ],
        out_specs=[
            pl.BlockSpec((tm, tn), lambda i, j: (i, j)),
            pl.BlockSpec((tm, LANES), lambda i, j: (i, 0)),
        ],
        out_shape=[
            jax.ShapeDtypeStruct((M, N), F32),
            jax.ShapeDtypeStruct((M, LANES), F32),
        ],
        scratch_shapes=[pltpu.VMEM((tm, K), BF16)],
        compiler_params=_params("parallel", "arbitrary"),
        name=name,
    )(x, nw.reshape(1, K), w_all, w2_all)


def _mm_res_norm_kernel(x_ref, w_ref, r_ref, nw_ref, o_ref, *, nk):
    chunks = _row_chunks(x_ref.shape[0])

    def body(first, last):
        for rows in chunks:
            _accumulate(o_ref, rows, x_ref[rows, :], w_ref, first)
            if last:
                o_ref[rows, :] = r_ref[rows, :] + _rms_rows(o_ref[rows, :], nw_ref[...])

    _by_position(pl.program_id(1), nk, body)


def _mm_res_norm(x, w_all, layer, resid, nw, *, name):
    M, K = x.shape
    N = w_all.shape[2]
    tm = _tile(M, ROW_TILES)
    tk = _tile(K, (1024, 512, 256, 128))
    nk = K // tk
    return pl.pallas_call(
        functools.partial(_mm_res_norm_kernel, nk=nk),
        grid=(M // tm, nk),
        in_specs=[
            pl.BlockSpec((tm, tk), lambda i, k: (i, k)),
            pl.BlockSpec((None, tk, N), lambda i, k: (layer, k, 0)),
            pl.BlockSpec((tm, N), lambda i, k: (i, 0)),
            pl.BlockSpec((1, N), lambda i, k: (0, 0)),
        ],
        out_specs=pl.BlockSpec((tm, N), lambda i, k: (i, 0), pipeline_mode=pl.Buffered(1)),
        out_shape=jax.ShapeDtypeStruct((M, N), F32),
        compiler_params=_params("parallel", "arbitrary"),
        name=name,
    )(x, w_all, resid, nw.reshape(1, N))


def _ffn_kernel(x_ref, nw_in_ref, wu_ref, wd_ref, nw_out_ref, o_ref, hn_ref, *, nf):
    chunks = _row_chunks(x_ref.shape[0])

    def body(first, last):
        for rows in chunks:
            if first:
                hn_ref[rows, :] = _rms_rows(x_ref[rows, :], nw_in_ref[...]).astype(BF16)
            h = jnp.dot(hn_ref[rows, :], wu_ref[...], preferred_element_type=F32)
            h = jnp.square(jnp.maximum(h, 0.0)).astype(BF16)
            _accumulate(o_ref, rows, h, wd_ref, first)
            if last:
                o_ref[rows, :] = x_ref[rows, :] + _rms_rows(o_ref[rows, :], nw_out_ref[...])

    _by_position(pl.program_id(1), nf, body)


def _ffn(x, nw_in, wu_all, wd_all, layer, nw_out, *, row0=0, nrows=None, name):
    D = x.shape[1]
    M = x.shape[0] if nrows is None else nrows
    F = wu_all.shape[2]
    tm = _tile(M, tuple(c for c in ROW_TILES if row0 % c == 0))
    blk0 = row0 // tm
    tf = _tile(F, (512, 256, 128))
    nf = F // tf
    vec = pl.BlockSpec((1, D), lambda i, f: (0, 0))
    return pl.pallas_call(
        functools.partial(_ffn_kernel, nf=nf),
        grid=(M // tm, nf),
        in_specs=[
            pl.BlockSpec((tm, D), lambda i, f: (blk0 + i, 0)),
            vec,
            pl.BlockSpec((None, D, tf), lambda i, f: (layer, 0, f)),
            pl.BlockSpec((None, tf, D), lambda i, f: (layer, f, 0)),
            vec,
        ],
        out_specs=pl.BlockSpec((tm, D), lambda i, f: (i, 0), pipeline_mode=pl.Buffered(1)),
        out_shape=jax.ShapeDtypeStruct((M, D), F32),
        scratch_shapes=[pltpu.VMEM((tm, D), BF16)],
        compiler_params=_params("parallel", "arbitrary"),
        name=name,
    )(x, nw_in.reshape(1, D), wu_all, wd_all, nw_out.reshape(1, D))


def _transpose_exact(a, eye):
    out = None
    r = a
    for _ in range(3):
        p = r.astype(BF16)
        r = r - p.astype(F32)
        t = lax.dot_general(eye, p, _NT, preferred_element_type=F32)
        out = t if out is None else out + t
    return out


def _ssd_kernel(*refs, L, D, G, P, zero_init, n_alias):
    z_ref, xr_ref, bcr_ref, dtr_ref = refs[:4]
    n_state = 0 if zero_init else 2
    (cwx_ref, cwbc_ref, cbx_ref, cbbc_ref, dtb_ref, alog_ref, dsk_ref, nw_ref,
     eye_ref) = refs[4 + n_state:13 + n_state]
    y_ref, hout_ref, cout_ref, ext_ref, h_ref, yscr_ref = refs[13 + n_state + n_alias:]
    c = pl.program_id(1)
    nc = pl.num_programs(1)
    N = SSD_STATE
    npairs = D // LANES
    pairs_per_group = npairs // G
    heads_per_pair = LANES // P

    @pl.when(c == 0)
    def _():
        if zero_init:
            ext_ref[0:SUBLANES, :] = jnp.zeros((SUBLANES, 2 * D), F32)
            h_ref[...] = jnp.zeros_like(h_ref)
        else:
            ext_ref[0:SUBLANES, :] = refs[4][...]
            h_ref[...] = refs[5][...]

    ext_ref[SUBLANES:SUBLANES + L, 0:D] = xr_ref[...]
    ext_ref[SUBLANES:SUBLANES + L, D:2 * D] = bcr_ref[...]

    rsub = lax.broadcasted_iota(jnp.int32, (L, D), 0) & (SUBLANES - 1)

    def conv(lo, w_ref, b_ref):
        u = ext_ref[:, lo:lo + D]
        acc = u[SUBLANES:] * w_ref[SSD_CONV - 1:SSD_CONV, :]
        for s in range(1, SSD_CONV):
            r = _roll_in_block(u, s)
            shifted = jnp.where(rsub >= s, r[SUBLANES:], r[:L])
            acc = acc + shifted * w_ref[SSD_CONV - 1 - s:SSD_CONV - s, :]
        return _silu(acc + b_ref[...])

    xs = conv(0, cwx_ref, cbx_ref)
    bc = conv(D, cwbc_ref, cbbc_ref)
    last_rows = ext_ref[L:L + SUBLANES, :]
    ext_ref[0:SUBLANES, :] = last_rows

    dt = _softplus(dtr_ref[...] + dtb_ref[...])
    a = dt * (-LOG2_E * jnp.exp(alog_ref[...]))
    row = lax.broadcasted_iota(jnp.int32, (L, LANES), 0)
    acs = a
    s = 1
    while s < L:
        acs = acs + jnp.where(row >= s, _roll_rows(acs, s), 0.0)
        s *= 2
    acs_t = _transpose_exact(acs, eye_ref[...])
    last = acs[L - 1:L, :]
    e_in = jnp.exp2(acs)
    e_tail = jnp.exp2(last - acs)
    e_all = jnp.exp2(last)

    lane = lax.broadcasted_iota(jnp.int32, (L, LANES), 1)
    tri = (lax.broadcasted_iota(jnp.int32, (L, L), 0)
           >= lax.broadcasted_iota(jnp.int32, (L, L), 1))
    srow = lax.broadcasted_iota(jnp.int32, (LANES, LANES), 0)

    def per_head(arr, j, rows=lane):
        out = None
        for hh in reversed(range(heads_per_pair)):
            h = j * heads_per_pair + hh
            col = arr[:, h:h + 1]
            out = col if out is None else jnp.where(rows < (hh + 1) * P, col, out)
        return out

    for g in range(G):
        bg = bc[:, g * N:(g + 1) * N].astype(BF16)
        cg = bc[:, (G + g) * N:(G + g + 1) * N].astype(BF16)
        cb = lax.dot_general(cg, bg, _NT, preferred_element_type=F32)
        cb = jnp.where(tri, cb, 0.0)
        for jj in range(pairs_per_group):
            j = g * pairs_per_group + jj
            sl = slice(j * LANES, (j + 1) * LANES)
            dx = per_head(dt, j) * xs[:, sl]
            dxb = dx.astype(BF16)
            ydiag = None
            for hh in reversed(range(heads_per_pair)):
                h = j * heads_per_pair + hh
                seg = jnp.minimum(acs[:, h:h + 1] - acs_t[h:h + 1, :], 0.0)
                m = (cb * jnp.exp2(seg)).astype(BF16)
                yh = jnp.dot(m, dxb, preferred_element_type=F32)
                ydiag = yh if ydiag is None else jnp.where(lane < (hh + 1) * P, yh, ydiag)
            hp = h_ref[j]
            ystate = lax.dot_general(cg, hp.astype(BF16), _NT, preferred_element_type=F32)
            yscr_ref[:, sl] = ydiag + ystate * per_head(e_in, j)
            dxw = (dx * per_head(e_tail, j)).astype(BF16)
            upd = lax.dot_general(dxw, bg, _TN, preferred_element_type=F32)
            scale = None
            for hh in reversed(range(heads_per_pair)):
                h = j * heads_per_pair + hh
                col = e_all[:, h:h + 1]
                scale = col if scale is None else jnp.where(srow < (hh + 1) * P, col, scale)
            h_ref[j] = hp * scale + upd

    y = (yscr_ref[...] + dsk_ref[...] * xs) * _silu(z_ref[...])
    gw = D // G
    for g in range(G):
        sl = slice(g * gw, (g + 1) * gw)
        seg = y[:, sl]
        ms = jnp.mean(seg * seg, axis=-1, keepdims=True)
        y_ref[:, sl] = (seg * lax.rsqrt(ms + EPS) * nw_ref[:, sl]).astype(y_ref.dtype)

    @pl.when(c == nc - 1)
    def _():
        hout_ref[...] = h_ref[...]
        cout_ref[...] = last_rows


def _ssd(proj, dtr, state, cw, cb, dtb, alog, dsk, nw, prev, *, layer, n_layers, row0, nseq, T, L,
         D, G, P, name):
    M = proj.shape[0]
    nc = T // L
    blk0 = row0 // L
    npairs = D // LANES
    zero_init = state is None

    def rows(col):
        return pl.BlockSpec((L, D), lambda b, c: (blk0 + b * nc + c, col))

    def whole(shape):
        return pl.BlockSpec(shape, lambda b, c: (0,) * len(shape))

    h_spec = pl.BlockSpec((None, None, npairs, LANES, SSD_STATE), lambda b, c: (layer, b, 0, 0, 0))
    c_spec = pl.BlockSpec((None, None, SUBLANES, 2 * D), lambda b, c: (layer, b, 0, 0))
    in_specs = [rows(0), rows(1), rows(2),
                pl.BlockSpec((L, LANES), lambda b, c: (blk0 + b * nc + c, 0))]
    args = [proj, proj, proj, dtr]
    if not zero_init:
        in_specs += [c_spec, h_spec]
        args += list(state)
    in_specs += [whole((SSD_CONV, D)), whole((SSD_CONV, D)), whole((1, D)), whole((1, D)),
                 whole((1, LANES)), whole((1, LANES)), whole((1, D)), whole((1, D)),
                 whole((LANES, LANES))]
    args += [cw[:, :D], cw[:, D:], cb[:, :D], cb[:, D:], dtb, alog, dsk, nw,
             jnp.eye(LANES, dtype=BF16)]
    aliases = {}
    for k, buf in enumerate(prev):
        if buf is not None:
            aliases[len(args)] = k
            in_specs.append(pl.BlockSpec(memory_space=pl.ANY))
            args.append(buf)
    return pl.pallas_call(
        functools.partial(_ssd_kernel, L=L, D=D, G=G, P=P, zero_init=zero_init,
                          n_alias=len(aliases)),
        grid=(nseq, nc),
        in_specs=in_specs,
        out_specs=[pl.BlockSpec((L, D), lambda b, c: (blk0 + b * nc + c, 0)), h_spec, c_spec],
        out_shape=[
            jax.ShapeDtypeStruct((M, 2 * D), BF16),
            jax.ShapeDtypeStruct((n_layers, nseq, npairs, LANES, SSD_STATE), F32),
            jax.ShapeDtypeStruct((n_layers, nseq, SUBLANES, 2 * D), F32),
        ],
        scratch_shapes=[
            pltpu.VMEM((SUBLANES + L, 2 * D), F32),
            pltpu.VMEM((npairs, LANES, SSD_STATE), F32),
            pltpu.VMEM((L, D), F32),
        ],
        input_output_aliases=aliases,
        compiler_params=_params("parallel", "arbitrary"),
        name=name,
    )(*args)


def _hgrn_kernel(*refs, L, D, zero_init, n_alias):
    q_ref, f_ref, i_ref, g_ref, lb_ref, nw_ref = refs[:6]
    o_ref, sout_ref, st_ref = refs[(6 if zero_init else 7) + n_alias:]
    c = pl.program_id(1)
    nc = pl.num_programs(1)
    H = D // HG_KEY

    @pl.when(c == 0)
    def _():
        if zero_init:
            st_ref[...] = jnp.zeros_like(st_ref)
        else:
            for h in range(H):
                st_ref[h] = refs[6][h].T

    lb = lb_ref[...]
    fz = f_ref[...]
    la = jnp.log(lb)
    lg = jnp.log1p(-lb) - (jnp.maximum(-fz, 0.0) + jnp.log(1.0 + jnp.exp(-jnp.abs(fz))))
    logf = jnp.maximum(la, lg) + jnp.log(1.0 + jnp.exp(-jnp.abs(la - lg)))
    q = q_ref[...]
    k = (1.0 - lb) * _sigmoid(-fz)
    v = i_ref[...]
    lf2 = logf * LOG2_E
    f = jnp.exp2(lf2)

    row = lax.broadcasted_iota(jnp.int32, (L, D), 0)
    r8 = row & (SUBLANES - 1)
    p = lf2
    t = lf2
    for s in (1, 2, 4):
        p = p + jnp.where(r8 >= s, _roll_in_block(p, s), 0.0)
        t = t + _roll_in_block(t, s)
    levels = []
    m = SUBLANES
    while m < L:
        second = (row & (2 * m - 1)) >= m
        ex = jnp.exp2(jnp.where(second, p, t - p))
        levels.append((m, (q * ex).astype(BF16), (k * ex).astype(BF16)))
        prev_t = _roll_rows(t, m)
        next_t = _roll_rows(t, L - m)
        p = p + jnp.where(second, prev_t, 0.0)
        t = t + jnp.where(second, prev_t, next_t)
        m *= 2
    q_in = (q * jnp.exp2(p)).astype(BF16)
    k_tail = (k * jnp.exp2(t - p)).astype(BF16)
    dec = jnp.exp2(t[0:1, :])
    vb = v.astype(BF16)

    rl = lax.broadcasted_iota(jnp.int32, (L, L), 0)
    cl = lax.broadcasted_iota(jnp.int32, (L, L), 1)
    lag_masks = [(cl == rl - d) & ((rl & (SUBLANES - 1)) >= d) for d in range(SUBLANES)]
    level_masks = [((rl >> ((2 * m).bit_length() - 1)) == (cl >> ((2 * m).bit_length() - 1)))
                   & ((rl & (2 * m - 1)) >= m) & ((cl & (2 * m - 1)) < m)
                   for (m, _, _) in levels]

    for h in range(H):
        sl = slice(h * HG_KEY, (h + 1) * HG_KEY)
        qh, fh = q[:, sl], f[:, sl]
        gd = k[:, sl]
        amat = jnp.where(lag_masks[0], jnp.sum(qh * gd, axis=-1, keepdims=True), 0.0)
        for d in range(1, SUBLANES):
            gd = fh * _roll_in_block(gd, 1)
            amat = jnp.where(lag_masks[d], jnp.sum(qh * gd, axis=-1, keepdims=True), amat)
        for (m, ql, kl), mask in zip(levels, level_masks):
            sc = lax.dot_general(ql[:, sl], kl[:, sl], _NT, preferred_element_type=F32)
            amat = jnp.where(mask, sc, amat)
        o = jnp.dot(amat.astype(BF16), vb[:, sl], preferred_element_type=F32)
        st = st_ref[h]
        o = o + lax.dot_general(q_in[:, sl], st.astype(BF16), _NT, preferred_element_type=F32)
        st_ref[h] = st * dec[:, sl] + lax.dot_general(vb[:, sl], k_tail[:, sl], _TN,
                                                      preferred_element_type=F32)
        ms = jnp.mean(o * o, axis=-1, keepdims=True)
        o = o * lax.rsqrt(ms + EPS)
        o_ref[:, sl] = (o * nw_ref[:, sl] * _silu(g_ref[:, sl])).astype(o_ref.dtype)

    @pl.when(c == nc - 1)
    def _():
        for h in range(H):
            sout_ref[h] = st_ref[h].T


def _hgrn(proj, lb, state, nw, prev, *, layer, n_layers, row0, nseq, T, L, D, name):
    nc = T // L
    blk0 = row0 // L
    H = D // HG_KEY
    zero_init = state is None

    def rows(col):
        return pl.BlockSpec((L, D), lambda b, c: (blk0 + b * nc + c, col))

    vec = pl.BlockSpec((1, D), lambda b, c: (0, 0))
    s_spec = pl.BlockSpec((None, None, H, HG_KEY, HG_KEY), lambda b, c: (layer, b, 0, 0, 0))
    in_specs = [rows(3), rows(4), rows(5), rows(6), vec, vec]
    args = [proj, proj, proj, proj, lb, nw]
    if not zero_init:
        in_specs.append(s_spec)
        args.append(state)
    aliases = {}
    for k, buf in enumerate(prev):
        if buf is not None:
            aliases[len(args)] = k
            in_specs.append(pl.BlockSpec(memory_space=pl.ANY))
            args.append(buf)
    return pl.pallas_call(
        functools.partial(_hgrn_kernel, L=L, D=D, zero_init=zero_init, n_alias=len(aliases)),
        grid=(nseq, nc),
        in_specs=in_specs,
        out_specs=[rows(1), s_spec],
        out_shape=[
            jax.ShapeDtypeStruct(prev[0].shape, prev[0].dtype),
            jax.ShapeDtypeStruct((n_layers, nseq, H, HG_KEY, HG_KEY), F32),
        ],
        scratch_shapes=[pltpu.VMEM((H, HG_KEY, HG_KEY), F32)],
        input_output_aliases=aliases,
        compiler_params=_params("parallel", "arbitrary"),
        name=name,
    )(*args)


def _pool_kernel(*refs, L, D, pos0, zero_init, n_alias):
    x_ref, nw0_ref, pw_ref, ps_ref, nw1_ref = refs[:5]
    xo_ref, tail_ref, ext_ref = refs[(5 if zero_init else 6) + n_alias:]
    c = pl.program_id(1)
    hist = POOL_BUF + 1

    @pl.when(c == 0)
    def _():
        if zero_init:
            ext_ref[0:hist, :] = jnp.zeros((hist, D), F32)
        else:
            ext_ref[0:hist, :] = refs[5][...]

    x = x_ref[...]
    ms = jnp.mean(x * x, axis=-1, keepdims=True)
    hn = x * lax.rsqrt(ms + EPS) * nw0_ref[...]
    ext_ref[hist:hist + L, :] = hn

    pos = pos0 + c * L + lax.broadcasted_iota(jnp.int32, (L, 1), 0)
    gw = D // len(POOL_WINDOWS)
    parts = []
    ss = None
    for gi, w in enumerate(POOL_WINDOWS):
        sl = slice(gi * gw, (gi + 1) * gw)
        s = ext_ref[:, sl]
        span = 1
        while span < w:
            s = s + _roll_rows(s, span)
            span *= 2
        cnt = jnp.minimum(pos + 1, w).astype(F32)
        pooled = s[hist:hist + L] / cnt - hn[:, sl]
        mixed = jnp.dot(pooled.astype(BF16), pw_ref[gi], preferred_element_type=F32) * ps_ref[:, sl]
        parts.append(mixed)
        sq = jnp.sum(mixed * mixed, axis=-1, keepdims=True)
        ss = sq if ss is None else ss + sq
    rs = lax.rsqrt(ss / D + EPS)
    for gi in range(len(POOL_WINDOWS)):
        sl = slice(gi * gw, (gi + 1) * gw)
        xo_ref[:, sl] = x[:, sl] + parts[gi] * rs * nw1_ref[:, sl]

    new_hist = ext_ref[L:L + hist, :]
    tail_ref[...] = new_hist
    ext_ref[0:hist, :] = new_hist


def _pool(x, buf, nw0, pw_all, ps, nw1, prev_tail, *, layer, n_layers, row0, nseq, T, L, D, pos0,
          name):
    nc = T // L
    blk0 = row0 // L
    hist = POOL_BUF + 1
    ng = len(POOL_WINDOWS)
    gw = D // ng
    zero_init = buf is None
    vec = pl.BlockSpec((1, D), lambda b, c: (0, 0))
    xrows = pl.BlockSpec((L, D), lambda b, c: (blk0 + b * nc + c, 0))
    t_spec = pl.BlockSpec((None, None, hist, D), lambda b, c: (layer, b, 0, 0))
    in_specs = [xrows, vec,
                pl.BlockSpec((None, ng, gw, gw), lambda b, c: (layer, 0, 0, 0)), vec, vec]
    args = [x, nw0, pw_all, ps, nw1]
    if not zero_init:
        in_specs.append(t_spec)
        args.append(buf)
    aliases = {0: 0}
    if prev_tail is not None:
        aliases[len(args)] = 1
        in_specs.append(pl.BlockSpec(memory_space=pl.ANY))
        args.append(prev_tail)
    return pl.pallas_call(
        functools.partial(_pool_kernel, L=L, D=D, pos0=pos0, zero_init=zero_init,
                          n_alias=len(aliases) - 1),
        grid=(nseq, nc),
        in_specs=in_specs,
        out_specs=[xrows, t_spec],
        out_shape=[
            jax.ShapeDtypeStruct(x.shape, F32),
            jax.ShapeDtypeStruct((n_layers, nseq, hist, D), F32),
        ],
        scratch_shapes=[pltpu.VMEM((hist + L, D), F32)],
        input_output_aliases=aliases,
        compiler_params=_params("parallel", "arbitrary"),
        name=name,
    )(*args)


def _pad_lanes(v):
    return jnp.pad(v.astype(F32), (0, LANES - v.shape[0])).reshape(1, LANES)


def kernel(x_prompt, x_sample, state_conv, state_ssd, state_hgrn, state_pool, norm_w, w_in,
           conv_w, conv_b, dt_bias, a_log, d_skip, ssd_norm_w, hg_norm_w, hg_lower_bounds,
           w_out, pool_w, pool_scale, w_ffn_up, w_ffn_down):
    B, T, D = x_prompt.shape
    Bs, Ts, _ = x_sample.shape
    depth = norm_w.shape[0]
    n_ab, n_c = w_in.shape[0], pool_w.shape[0]
    heads = dt_bias.shape[1]
    P = D // heads
    conv_dim = conv_w.shape[-1]
    G = (conv_dim - D) // (2 * SSD_STATE)
    assert conv_dim == 2 * D and LANES % P == 0 and heads <= LANES and D % (G * LANES) == 0
    assert Ts >= POOL_BUF + 1 and T >= POOL_BUF + 1
    Mp, Ms = B * T, Bs * Ts
    npairs = D // LANES
    keep = SSD_CONV - 1

    segs = (dict(row0=0, nseq=B, T=T), dict(row0=Mp, nseq=Bs, T=Ts))
    ssd_in = (None, (jnp.pad(state_conv.astype(F32), ((0, 0), (0, 0), (SUBLANES - keep, 0), (0, 0))),
                     state_ssd.astype(F32).reshape(n_ab, Bs, npairs, LANES, SSD_STATE)))
    hg_in = (None, state_hgrn.astype(F32))
    pool_in = (None, jnp.pad(state_pool.astype(F32), ((0, 0), (0, 0), (1, 0), (0, 0))))
    pos0 = (0, PAST_LEN)

    x = jnp.concatenate([x_prompt.reshape(Mp, D), x_sample.reshape(Ms, D)], axis=0)

    lbs = jnp.cumsum(jax.nn.softmax(hg_lower_bounds.astype(F32), axis=0), axis=0)
    lbs = lbs - lbs[0]

    w_main_b = jnp.concatenate([w_in[:, :, :3 * D], w_in[:, :, 3 * D + heads:]], axis=2).astype(BF16)
    w_dt_b = jnp.pad(w_in[:, :, 3 * D:3 * D + heads],
                     ((0, 0), (0, 0), (0, LANES - heads))).astype(BF16)
    w_out_b = w_out.astype(BF16)
    w_up_b = w_ffn_up.astype(BF16)
    w_down_b = w_ffn_down.astype(BF16)
    pool_w_b = pool_w.astype(BF16)

    ssd_st, conv_st, hg_st, pool_st = [None, None], [None, None], [None, None], [None, None]
    for layer in range(depth):
        j = layer // 2
        nw = norm_w[layer].astype(F32)
        if layer % 2 == 0:
            proj, dtr = _norm_mm(x, nw[0], w_main_b, w_dt_b, j, name=f"in_proj_{layer}")
            dsk = jnp.repeat(d_skip[j].astype(F32), P).reshape(1, D)
            mixed = None
            for si, seg in enumerate(segs):
                mixed, ssd_st[si], conv_st[si] = _ssd(
                    proj, dtr, ssd_in[si], conv_w[j].astype(F32),
                    conv_b[j].astype(F32).reshape(1, conv_dim), _pad_lanes(dt_bias[j]),
                    _pad_lanes(a_log[j]), dsk, ssd_norm_w[j].astype(F32).reshape(1, D),
                    (mixed, ssd_st[si], conv_st[si]), layer=j, n_layers=n_ab,
                    L=_tile(seg["T"], (128, 64, 32, 16)), D=D, G=G, P=P,
                    name=f"ssd_{layer}_{si}", **seg)
            for si, seg in enumerate(segs):
                mixed, hg_st[si] = _hgrn(
                    proj, lbs[j].reshape(1, D), hg_in[si], hg_norm_w[j].astype(F32).reshape(1, D),
                    (mixed, hg_st[si]), layer=j, n_layers=n_ab,
                    L=_tile(seg["T"], (128, 64, 32, 16)), D=D, name=f"hgrn_{layer}_{si}", **seg)
            x = _mm_res_norm(mixed, w_out_b, j, x, nw[1], name=f"out_proj_{layer}")
        else:
            for si, seg in enumerate(segs):
                x, pool_st[si] = _pool(
                    x, pool_in[si], nw[0].reshape(1, D), pool_w_b,
                    pool_scale[j].astype(F32).reshape(1, D), nw[1].reshape(1, D), pool_st[si],
                    layer=j, n_layers=n_c, L=_tile(seg["T"], (256, 128, 64, 32, 16)), D=D,
                    pos0=pos0[si], name=f"pool_{layer}_{si}", **seg)
        mlp = functools.partial(_ffn, x, nw[2], w_up_b, w_down_b, layer, nw[3])
        if layer < depth - 1:
            x = mlp(name=f"ffn_{layer}")
        else:
            y_p = mlp(row0=0, nrows=Mp, name=f"ffn_{layer}_0")
            y_s = mlp(row0=Mp, nrows=Ms, name=f"ffn_{layer}_1")

    dt_out = x_prompt.dtype
    nseqs = (B, Bs)
    outs = [y_p.reshape(B, T, D), y_s.reshape(Bs, Ts, D)]
    for si in range(2):
        outs += [conv_st[si][:, :, SUBLANES - keep:, :],
                 ssd_st[si].reshape(n_ab, nseqs[si], heads, P, SSD_STATE),
                 hg_st[si],
                 pool_st[si][:, :, 1:, :]]
    return tuple(o.astype(dt_out) for o in outs)
```

```python
import functools

import jax
import jax.numpy as jnp
from jax import lax
from jax.experimental import pallas as pl
from jax.experimental.pallas import tpu as pltpu

F32 = jnp.float32
BF16 = jnp.bfloat16
EPS = 1e-6
PAST_LEN = 4096
POOL_WINDOWS = (2, 4, 8, 16)
POOL_BUF = max(POOL_WINDOWS) - 1
SSD_STATE = 128
SSD_CONV = 4
HG_KEY = 128
LOG2_E = 1.4426950408889634
LANES = 128
SUBLANES = 8
VMEM_LIMIT_BYTES = 56 * 1024 * 1024

_NT = (((1,), (1,)), ((), ()))
_TN = (((0,), (0,)), ((), ()))


def _tile(n, candidates):
    for c in candidates:
        if n % c == 0:
            return c
    raise ValueError(f"no tile for {n} in {candidates}")


def _params(*sem):
    return pltpu.CompilerParams(dimension_semantics=sem, vmem_limit_bytes=VMEM_LIMIT_BYTES)


def _softplus(x):
    return jnp.maximum(x, 0.0) + jnp.log(1.0 + jnp.exp(-jnp.abs(x)))


def _sigmoid(x):
    return 0.5 * jnp.tanh(0.5 * x) + 0.5


def _silu(x):
    h = 0.5 * x
    return h * jnp.tanh(h) + h


def _roll_rows(x, shift):
    return pltpu.roll(x, shift, 0)


def _roll_in_block(x, shift):
    rows, width = x.shape
    x3 = x.reshape(rows // SUBLANES, SUBLANES, width)
    return pltpu.roll(x3, shift, 1).reshape(rows, width)


ROW_TILES = (1280, 1024, 512, 256, 192, 128, 64, 32, 16)
ROW_CHUNK = 320
ACC_COLS = 512


def _row_chunks(tm):
    rc = ROW_CHUNK if tm % ROW_CHUNK == 0 else tm
    return [slice(r0, r0 + rc) for r0 in range(0, tm, rc)]


def _rms_rows(x, w):
    ms = jnp.mean(x * x, axis=-1, keepdims=True)
    return x * lax.rsqrt(ms + EPS) * w


def _accumulate(acc_ref, rows, lhs, w_ref, first):
    n = acc_ref.shape[1]
    step = min(ACC_COLS, n)
    for n0 in range(0, n, step):
        cols = slice(n0, n0 + step)
        d = jnp.dot(lhs, w_ref[:, cols], preferred_element_type=F32)
        if first:
            acc_ref[rows, cols] = d
        else:
            acc_ref[rows, cols] += d


def _by_position(step, nsteps, body):
    if nsteps == 1:
        body(True, True)
        return
    pl.when(step == 0)(lambda: body(True, False))
    if nsteps > 2:
        pl.when(jnp.logical_and(step > 0, step < nsteps - 1))(lambda: body(False, False))
    pl.when(step == nsteps - 1)(lambda: body(False, True))


def _norm_mm_kernel(x_ref, nw_ref, w_ref, w2_ref, o_ref, o2_ref, hn_ref, *, nj):
    chunks = _row_chunks(x_ref.shape[0])

    def body(first, last):
        del last
        for rows in chunks:
            if first:
                hn_ref[rows, :] = _rms_rows(x_ref[rows, :], nw_ref[...]).astype(BF16)
                o2_ref[rows, :] = jnp.dot(hn_ref[rows, :], w2_ref[...],
                                          preferred_element_type=F32)
            o_ref[rows, :] = jnp.dot(hn_ref[rows, :], w_ref[...],
                                     preferred_element_type=F32).astype(o_ref.dtype)

    j = pl.program_id(1)
    pl.when(j == 0)(lambda: body(True, False))
    if nj > 1:
        pl.when(j > 0)(lambda: body(False, False))


def _norm_mm(x, nw, w_all, w2_all, layer, *, name):
    M, K = x.shape
    N = w_all.shape[2]
    tm = _tile(M, ROW_TILES)
    tn = _tile(N, (1024, 512, 256, 128))
    return pl.pallas_call(
        functools.partial(_norm_mm_kernel, nj=N // tn),
        grid=(M // tm, N // tn),
        in_specs=[
            pl.BlockSpec((tm, K), lambda i, j: (i, 0)),
            pl.BlockSpec((1, K), lambda i, j: (0, 0)),
            pl.BlockSpec((None, K, tn), lambda i, j: (layer, 0, j)),
            pl.BlockSpec((None, K, LANES), lambda i, j: (layer, 0, 0)),
        ],
        out_specs=[
            pl.BlockSpec((tm, tn), lambda i, j: (i, j)),
            pl.BlockSpec((tm, LANES), lambda i, j: (i, 0)),
        ],
        out_shape=[
            jax.ShapeDtypeStruct((M, N), F32),
            jax.ShapeDtypeStruct((M, LANES), F32),
        ],
        scratch_shapes=[pltpu.VMEM((tm, K), BF16)],
        compiler_params=_params("parallel", "arbitrary"),
        name=name,
    )(x, nw.reshape(1, K), w_all, w2_all)


def _mm_res_norm_kernel(x_ref, w_ref, r_ref, nw_ref, o_ref, *, nk):
    chunks = _row_chunks(x_ref.shape[0])

    def body(first, last):
        for rows in chunks:
            _accumulate(o_ref, rows, x_ref[rows, :], w_ref, first)
            if last:
                o_ref[rows, :] = r_ref[rows, :] + _rms_rows(o_ref[rows, :], nw_ref[...])

    _by_position(pl.program_id(1), nk, body)


def _mm_res_norm(x, w_all, layer, resid, nw, *, name):
    M, K = x.shape
    N = w_all.shape[2]
    tm = _tile(M, ROW_TILES)
    tk = _tile(K, (1024, 512, 256, 128))
    nk = K // tk
    return pl.pallas_call(
        functools.partial(_mm_res_norm_kernel, nk=nk),
        grid=(M // tm, nk),
        in_specs=[
            pl.BlockSpec((tm, tk), lambda i, k: (i, k)),
            pl.BlockSpec((None, tk, N), lambda i, k: (layer, k, 0)),
            pl.BlockSpec((tm, N), lambda i, k: (i, 0)),
            pl.BlockSpec((1, N), lambda i, k: (0, 0)),
        ],
        out_specs=pl.BlockSpec((tm, N), lambda i, k: (i, 0), pipeline_mode=pl.Buffered(1)),
        out_shape=jax.ShapeDtypeStruct((M, N), F32),
        compiler_params=_params("parallel", "arbitrary"),
        name=name,
    )(x, w_all, resid, nw.reshape(1, N))


def _ffn_kernel(x_ref, nw_in_ref, wu_ref, wd_ref, nw_out_ref, o_ref, hn_ref, *, nf):
    chunks = _row_chunks(x_ref.shape[0])

    def body(first, last):
        for rows in chunks:
            if first:
                hn_ref[rows, :] = _rms_rows(x_ref[rows, :], nw_in_ref[...]).astype(BF16)
            h = jnp.dot(hn_ref[rows, :], wu_ref[...], preferred_element_type=F32)
            h = jnp.square(jnp.maximum(h, 0.0)).astype(BF16)
            _accumulate(o_ref, rows, h, wd_ref, first)
            if last:
                o_ref[rows, :] = x_ref[rows, :] + _rms_rows(o_ref[rows, :], nw_out_ref[...])

    _by_position(pl.program_id(1), nf, body)


def _ffn(x, nw_in, wu_all, wd_all, layer, nw_out, *, row0=0, nrows=None, name):
    D = x.shape[1]
    M = x.shape[0] if nrows is None else nrows
    F = wu_all.shape[2]
    tm = _tile(M, tuple(c for c in ROW_TILES if row0 % c == 0))
    blk0 = row0 // tm
    tf = _tile(F, (1024, 512, 256, 128))
    nf = F // tf
    vec = pl.BlockSpec((1, D), lambda i, f: (0, 0))
    return pl.pallas_call(
        functools.partial(_ffn_kernel, nf=nf),
        grid=(M // tm, nf),
        in_specs=[
            pl.BlockSpec((tm, D), lambda i, f: (blk0 + i, 0)),
            vec,
            pl.BlockSpec((None, D, tf), lambda i, f: (layer, 0, f)),
            pl.BlockSpec((None, tf, D), lambda i, f: (layer, f, 0)),
            vec,
        ],
        out_specs=pl.BlockSpec((tm, D), lambda i, f: (i, 0), pipeline_mode=pl.Buffered(1)),
        out_shape=jax.ShapeDtypeStruct((M, D), F32),
        scratch_shapes=[pltpu.VMEM((tm, D), BF16)],
        compiler_params=_params("parallel", "arbitrary"),
        name=name,
    )(x, nw_in.reshape(1, D), wu_all, wd_all, nw_out.reshape(1, D))


def _transpose_exact(a, eye):
    out = None
    r = a
    for _ in range(3):
        p = r.astype(BF16)
        r = r - p.astype(F32)
        t = lax.dot_general(eye, p, _NT, preferred_element_type=F32)
        out = t if out is None else out + t
    return out


def _ssd_kernel(*refs, L, D, G, P, zero_init, n_alias):
    z_ref, xr_ref, bcr_ref, dtr_ref = refs[:4]
    n_state = 0 if zero_init else 2
    (cwx_ref, cwbc_ref, cbx_ref, cbbc_ref, dtb_ref, alog_ref, dsk_ref, nw_ref,
     eye_ref) = refs[4 + n_state:13 + n_state]
    y_ref, hout_ref, cout_ref, ext_ref, h_ref, yscr_ref = refs[13 + n_state + n_alias:]
    c = pl.program_id(1)
    nc = pl.num_programs(1)
    N = SSD_STATE
    npairs = D // LANES
    pairs_per_group = npairs // G
    heads_per_pair = LANES // P

    @pl.when(c == 0)
    def _():
        if zero_init:
            ext_ref[0:SUBLANES, :] = jnp.zeros((SUBLANES, 2 * D), F32)
            h_ref[...] = jnp.zeros_like(h_ref)
        else:
            ext_ref[0:SUBLANES, :] = refs[4][...]
            h_ref[...] = refs[5][...]

    ext_ref[SUBLANES:SUBLANES + L, 0:D] = xr_ref[...]
    ext_ref[SUBLANES:SUBLANES + L, D:2 * D] = bcr_ref[...]

    rsub = lax.broadcasted_iota(jnp.int32, (L, D), 0) & (SUBLANES - 1)

    def conv(lo, w_ref, b_ref):
        u = ext_ref[:, lo:lo + D]
        acc = u[SUBLANES:] * w_ref[SSD_CONV - 1:SSD_CONV, :]
        for s in range(1, SSD_CONV):
            r = _roll_in_block(u, s)
            shifted = jnp.where(rsub >= s, r[SUBLANES:], r[:L])
            acc = acc + shifted * w_ref[SSD_CONV - 1 - s:SSD_CONV - s, :]
        return _silu(acc + b_ref[...])

    xs = conv(0, cwx_ref, cbx_ref)
    bc = conv(D, cwbc_ref, cbbc_ref)
    last_rows = ext_ref[L:L + SUBLANES, :]
    ext_ref[0:SUBLANES, :] = last_rows

    dt = _softplus(dtr_ref[...] + dtb_ref[...])
    a = dt * (-LOG2_E * jnp.exp(alog_ref[...]))
    row = lax.broadcasted_iota(jnp.int32, (L, LANES), 0)
    acs = a
    s = 1
    while s < L:
        acs = acs + jnp.where(row >= s, _roll_rows(acs, s), 0.0)
        s *= 2
    acs_t = _transpose_exact(acs, eye_ref[...])
    last = acs[L - 1:L, :]
    e_in = jnp.exp2(acs)
    e_tail = jnp.exp2(last - acs)
    e_all = jnp.exp2(last)

    lane = lax.broadcasted_iota(jnp.int32, (L, LANES), 1)
    tri = (lax.broadcasted_iota(jnp.int32, (L, L), 0)
           >= lax.broadcasted_iota(jnp.int32, (L, L), 1))
    srow = lax.broadcasted_iota(jnp.int32, (LANES, LANES), 0)

    def per_head(arr, j, rows=lane):
        out = None
        for hh in reversed(range(heads_per_pair)):
            h = j * heads_per_pair + hh
            col = arr[:, h:h + 1]
            out = col if out is None else jnp.where(rows < (hh + 1) * P, col, out)
        return out

    for g in range(G):
        bg = bc[:, g * N:(g + 1) * N].astype(BF16)
        cg = bc[:, (G + g) * N:(G + g + 1) * N].astype(BF16)
        cb = lax.dot_general(cg, bg, _NT, preferred_element_type=F32)
        cb = jnp.where(tri, cb, 0.0)
        for jj in range(pairs_per_group):
            j = g * pairs_per_group + jj
            sl = slice(j * LANES, (j + 1) * LANES)
            dx = per_head(dt, j) * xs[:, sl]
            dxb = dx.astype(BF16)
            ydiag = None
            for hh in reversed(range(heads_per_pair)):
                h = j * heads_per_pair + hh
                seg = jnp.minimum(acs[:, h:h + 1] - acs_t[h:h + 1, :], 0.0)
                m = (cb * jnp.exp2(seg)).astype(BF16)
                yh = jnp.dot(m, dxb, preferred_element_type=F32)
                ydiag = yh if ydiag is None else jnp.where(lane < (hh + 1) * P, yh, ydiag)
            hp = h_ref[j]
            ystate = lax.dot_general(cg, hp.astype(BF16), _NT, preferred_element_type=F32)
            yscr_ref[:, sl] = ydiag + ystate * per_head(e_in, j)
            dxw = (dx * per_head(e_tail, j)).astype(BF16)
            upd = lax.dot_general(dxw, bg, _TN, preferred_element_type=F32)
            scale = None
            for hh in reversed(range(heads_per_pair)):
                h = j * heads_per_pair + hh
                col = e_all[:, h:h + 1]
                scale = col if scale is None else jnp.where(srow < (hh + 1) * P, col, scale)
            h_ref[j] = hp * scale + upd

    y = (yscr_ref[...] + dsk_ref[...] * xs) * _silu(z_ref[...])
    gw = D // G
    for g in range(G):
        sl = slice(g * gw, (g + 1) * gw)
        seg = y[:, sl]
        ms = jnp.mean(seg * seg, axis=-1, keepdims=True)
        y_ref[:, sl] = (seg * lax.rsqrt(ms + EPS) * nw_ref[:, sl]).astype(y_ref.dtype)

    @pl.when(c == nc - 1)
    def _():
        hout_ref[...] = h_ref[...]
        cout_ref[...] = last_rows


def _ssd(proj, dtr, state, cw, cb, dtb, alog, dsk, nw, prev, *, layer, n_layers, row0, nseq, T, L,
         D, G, P, name):
    M = proj.shape[0]
    nc = T // L
    blk0 = row0 // L
    npairs = D // LANES
    zero_init = state is None

    def rows(col):
        return pl.BlockSpec((L, D), lambda b, c: (blk0 + b * nc + c, col))

    def whole(shape):
        return pl.BlockSpec(shape, lambda b, c: (0,) * len(shape))

    h_spec = pl.BlockSpec((None, None, npairs, LANES, SSD_STATE), lambda b, c: (layer, b, 0, 0, 0))
    c_spec = pl.BlockSpec((None, None, SUBLANES, 2 * D), lambda b, c: (layer, b, 0, 0))
    in_specs = [rows(0), rows(1), rows(2),
                pl.BlockSpec((L, LANES), lambda b, c: (blk0 + b * nc + c, 0))]
    args = [proj, proj, proj, dtr]
    if not zero_init:
        in_specs += [c_spec, h_spec]
        args += list(state)
    in_specs += [whole((SSD_CONV, D)), whole((SSD_CONV, D)), whole((1, D)), whole((1, D)),
                 whole((1, LANES)), whole((1, LANES)), whole((1, D)), whole((1, D)),
                 whole((LANES, LANES))]
    args += [cw[:, :D], cw[:, D:], cb[:, :D], cb[:, D:], dtb, alog, dsk, nw,
             jnp.eye(LANES, dtype=BF16)]
    aliases = {}
    for k, buf in enumerate(prev):
        if buf is not None:
            aliases[len(args)] = k
            in_specs.append(pl.BlockSpec(memory_space=pl.ANY))
            args.append(buf)
    return pl.pallas_call(
        functools.partial(_ssd_kernel, L=L, D=D, G=G, P=P, zero_init=zero_init,
                          n_alias=len(aliases)),
        grid=(nseq, nc),
        in_specs=in_specs,
        out_specs=[pl.BlockSpec((L, D), lambda b, c: (blk0 + b * nc + c, 0)), h_spec, c_spec],
        out_shape=[
            jax.ShapeDtypeStruct((M, 2 * D), BF16),
            jax.ShapeDtypeStruct((n_layers, nseq, npairs, LANES, SSD_STATE), F32),
            jax.ShapeDtypeStruct((n_layers, nseq, SUBLANES, 2 * D), F32),
        ],
        scratch_shapes=[
            pltpu.VMEM((SUBLANES + L, 2 * D), F32),
            pltpu.VMEM((npairs, LANES, SSD_STATE), F32),
            pltpu.VMEM((L, D), F32),
        ],
        input_output_aliases=aliases,
        compiler_params=_params("parallel", "arbitrary"),
        name=name,
    )(*args)


def _hgrn_kernel(*refs, L, D, zero_init, n_alias):
    q_ref, f_ref, i_ref, g_ref, lb_ref, nw_ref = refs[:6]
    o_ref, sout_ref, st_ref = refs[(6 if zero_init else 7) + n_alias:]
    c = pl.program_id(1)
    nc = pl.num_programs(1)
    H = D // HG_KEY

    @pl.when(c == 0)
    def _():
        if zero_init:
            st_ref[...] = jnp.zeros_like(st_ref)
        else:
            for h in range(H):
                st_ref[h] = refs[6][h].T

    lb = lb_ref[...]
    fz = f_ref[...]
    la = jnp.log(lb)
    lg = jnp.log1p(-lb) - (jnp.maximum(-fz, 0.0) + jnp.log(1.0 + jnp.exp(-jnp.abs(fz))))
    logf = jnp.maximum(la, lg) + jnp.log(1.0 + jnp.exp(-jnp.abs(la - lg)))
    q = q_ref[...]
    k = (1.0 - lb) * _sigmoid(-fz)
    v = i_ref[...]
    lf2 = logf * LOG2_E
    f = jnp.exp2(lf2)

    row = lax.broadcasted_iota(jnp.int32, (L, D), 0)
    r8 = row & (SUBLANES - 1)
    p = lf2
    t = lf2
    for s in (1, 2, 4):
        p = p + jnp.where(r8 >= s, _roll_in_block(p, s), 0.0)
        t = t + _roll_in_block(t, s)
    levels = []
    m = SUBLANES
    while m < L:
        second = (row & (2 * m - 1)) >= m
        ex = jnp.exp2(jnp.where(second, p, t - p))
        levels.append((m, (q * ex).astype(BF16), (k * ex).astype(BF16)))
        prev_t = _roll_rows(t, m)
        next_t = _roll_rows(t, L - m)
        p = p + jnp.where(second, prev_t, 0.0)
        t = t + jnp.where(second, prev_t, next_t)
        m *= 2
    q_in = (q * jnp.exp2(p)).astype(BF16)
    k_tail = (k * jnp.exp2(t - p)).astype(BF16)
    dec = jnp.exp2(t[0:1, :])
    vb = v.astype(BF16)

    rl = lax.broadcasted_iota(jnp.int32, (L, L), 0)
    cl = lax.broadcasted_iota(jnp.int32, (L, L), 1)
    lag_masks = [(cl == rl - d) & ((rl & (SUBLANES - 1)) >= d) for d in range(SUBLANES)]
    level_masks = [((rl >> ((2 * m).bit_length() - 1)) == (cl >> ((2 * m).bit_length() - 1)))
                   & ((rl & (2 * m - 1)) >= m) & ((cl & (2 * m - 1)) < m)
                   for (m, _, _) in levels]

    for h in range(H):
        sl = slice(h * HG_KEY, (h + 1) * HG_KEY)
        qh, fh = q[:, sl], f[:, sl]
        gd = k[:, sl]
        amat = jnp.where(lag_masks[0], jnp.sum(qh * gd, axis=-1, keepdims=True), 0.0)
        for d in range(1, SUBLANES):
            gd = fh * _roll_in_block(gd, 1)
            amat = jnp.where(lag_masks[d], jnp.sum(qh * gd, axis=-1, keepdims=True), amat)
        for (m, ql, kl), mask in zip(levels, level_masks):
            sc = lax.dot_general(ql[:, sl], kl[:, sl], _NT, preferred_element_type=F32)
            amat = jnp.where(mask, sc, amat)
        o = jnp.dot(amat.astype(BF16), vb[:, sl], preferred_element_type=F32)
        st = st_ref[h]
        o = o + lax.dot_general(q_in[:, sl], st.astype(BF16), _NT, preferred_element_type=F32)
        st_ref[h] = st * dec[:, sl] + lax.dot_general(vb[:, sl], k_tail[:, sl], _TN,
                                                      preferred_element_type=F32)
        ms = jnp.mean(o * o, axis=-1, keepdims=True)
        o = o * lax.rsqrt(ms + EPS)
        o_ref[:, sl] = (o * nw_ref[:, sl] * _silu(g_ref[:, sl])).astype(o_ref.dtype)

    @pl.when(c == nc - 1)
    def _():
        for h in range(H):
            sout_ref[h] = st_ref[h].T


def _hgrn(proj, lb, state, nw, prev, *, layer, n_layers, row0, nseq, T, L, D, name):
    nc = T // L
    blk0 = row0 // L
    H = D // HG_KEY
    zero_init = state is None

    def rows(col):
        return pl.BlockSpec((L, D), lambda b, c: (blk0 + b * nc + c, col))

    vec = pl.BlockSpec((1, D), lambda b, c: (0, 0))
    s_spec = pl.BlockSpec((None, None, H, HG_KEY, HG_KEY), lambda b, c: (layer, b, 0, 0, 0))
    in_specs = [rows(3), rows(4), rows(5), rows(6), vec, vec]
    args = [proj, proj, proj, proj, lb, nw]
    if not zero_init:
        in_specs.append(s_spec)
        args.append(state)
    aliases = {}
    for k, buf in enumerate(prev):
        if buf is not None:
            aliases[len(args)] = k
            in_specs.append(pl.BlockSpec(memory_space=pl.ANY))
            args.append(buf)
    return pl.pallas_call(
        functools.partial(_hgrn_kernel, L=L, D=D, zero_init=zero_init, n_alias=len(aliases)),
        grid=(nseq, nc),
        in_specs=in_specs,
        out_specs=[rows(1), s_spec],
        out_shape=[
            jax.ShapeDtypeStruct(prev[0].shape, prev[0].dtype),
            jax.ShapeDtypeStruct((n_layers, nseq, H, HG_KEY, HG_KEY), F32),
        ],
        scratch_shapes=[pltpu.VMEM((H, HG_KEY, HG_KEY), F32)],
        input_output_aliases=aliases,
        compiler_params=_params("parallel", "arbitrary"),
        name=name,
    )(*args)


def _pool_kernel(*refs, L, D, pos0, zero_init, n_alias):
    x_ref, nw0_ref, pw_ref, ps_ref, nw1_ref = refs[:5]
    xo_ref, tail_ref, ext_ref = refs[(5 if zero_init else 6) + n_alias:]
    c = pl.program_id(1)
    hist = POOL_BUF + 1

    @pl.when(c == 0)
    def _():
        if zero_init:
            ext_ref[0:hist, :] = jnp.zeros((hist, D), F32)
        else:
            ext_ref[0:hist, :] = refs[5][...]

    x = x_ref[...]
    ms = jnp.mean(x * x, axis=-1, keepdims=True)
    hn = x * lax.rsqrt(ms + EPS) * nw0_ref[...]
    ext_ref[hist:hist + L, :] = hn

    pos = pos0 + c * L + lax.broadcasted_iota(jnp.int32, (L, 1), 0)
    gw = D // len(POOL_WINDOWS)
    parts = []
    ss = None
    for gi, w in enumerate(POOL_WINDOWS):
        sl = slice(gi * gw, (gi + 1) * gw)
        s = ext_ref[:, sl]
        span = 1
        while span < w:
            s = s + _roll_rows(s, span)
            span *= 2
        cnt = jnp.minimum(pos + 1, w).astype(F32)
        pooled = s[hist:hist + L] / cnt - hn[:, sl]
        mixed = jnp.dot(pooled.astype(BF16), pw_ref[gi], preferred_element_type=F32) * ps_ref[:, sl]
        parts.append(mixed)
        sq = jnp.sum(mixed * mixed, axis=-1, keepdims=True)
        ss = sq if ss is None else ss + sq
    rs = lax.rsqrt(ss / D + EPS)
    for gi in range(len(POOL_WINDOWS)):
        sl = slice(gi * gw, (gi + 1) * gw)
        xo_ref[:, sl] = x[:, sl] + parts[gi] * rs * nw1_ref[:, sl]

    new_hist = ext_ref[L:L + hist, :]
    tail_ref[...] = new_hist
    ext_ref[0:hist, :] = new_hist


def _pool(x, buf, nw0, pw_all, ps, nw1, prev_tail, *, layer, n_layers, row0, nseq, T, L, D, pos0,
          name):
    nc = T // L
    blk0 = row0 // L
    hist = POOL_BUF + 1
    ng = len(POOL_WINDOWS)
    gw = D // ng
    zero_init = buf is None
    vec = pl.BlockSpec((1, D), lambda b, c: (0, 0))
    xrows = pl.BlockSpec((L, D), lambda b, c: (blk0 + b * nc + c, 0))
    t_spec = pl.BlockSpec((None, None, hist, D), lambda b, c: (layer, b, 0, 0))
    in_specs = [xrows, vec,
                pl.BlockSpec((None, ng, gw, gw), lambda b, c: (layer, 0, 0, 0)), vec, vec]
    args = [x, nw0, pw_all, ps, nw1]
    if not zero_init:
        in_specs.append(t_spec)
        args.append(buf)
    aliases = {0: 0}
    if prev_tail is not None:
        aliases[len(args)] = 1
        in_specs.append(pl.BlockSpec(memory_space=pl.ANY))
        args.append(prev_tail)
    return pl.pallas_call(
        functools.partial(_pool_kernel, L=L, D=D, pos0=pos0, zero_init=zero_init,
                          n_alias=len(aliases) - 1),
        grid=(nseq, nc),
        in_specs=in_specs,
        out_specs=[xrows, t_spec],
        out_shape=[
            jax.ShapeDtypeStruct(x.shape, F32),
            jax.ShapeDtypeStruct((n_layers, nseq, hist, D), F32),
        ],
        scratch_shapes=[pltpu.VMEM((hist + L, D), F32)],
        input_output_aliases=aliases,
        compiler_params=_params("parallel", "arbitrary"),
        name=name,
    )(*args)


def _pad_lanes(v):
    return jnp.pad(v.astype(F32), (0, LANES - v.shape[0])).reshape(1, LANES)


def kernel(x_prompt, x_sample, state_conv, state_ssd, state_hgrn, state_pool, norm_w, w_in,
           conv_w, conv_b, dt_bias, a_log, d_skip, ssd_norm_w, hg_norm_w, hg_lower_bounds,
           w_out, pool_w, pool_scale, w_ffn_up, w_ffn_down):
    B, T, D = x_prompt.shape
    Bs, Ts, _ = x_sample.shape
    depth = norm_w.shape[0]
    n_ab, n_c = w_in.shape[0], pool_w.shape[0]
    heads = dt_bias.shape[1]
    P = D // heads
    conv_dim = conv_w.shape[-1]
    G = (conv_dim - D) // (2 * SSD_STATE)
    assert conv_dim == 2 * D and LANES % P == 0 and heads <= LANES and D % (G * LANES) == 0
    assert Ts >= POOL_BUF + 1 and T >= POOL_BUF + 1
    Mp, Ms = B * T, Bs * Ts
    npairs = D // LANES
    keep = SSD_CONV - 1

    segs = (dict(row0=0, nseq=B, T=T), dict(row0=Mp, nseq=Bs, T=Ts))
    ssd_in = (None, (jnp.pad(state_conv.astype(F32), ((0, 0), (0, 0), (SUBLANES - keep, 0), (0, 0))),
                     state_ssd.astype(F32).reshape(n_ab, Bs, npairs, LANES, SSD_STATE)))
    hg_in = (None, state_hgrn.astype(F32))
    pool_in = (None, jnp.pad(state_pool.astype(F32), ((0, 0), (0, 0), (1, 0), (0, 0))))
    pos0 = (0, PAST_LEN)

    x = jnp.concatenate([x_prompt.reshape(Mp, D), x_sample.reshape(Ms, D)], axis=0)

    lbs = jnp.cumsum(jax.nn.softmax(hg_lower_bounds.astype(F32), axis=0), axis=0)
    lbs = lbs - lbs[0]

    w_main_b = jnp.concatenate([w_in[:, :, :3 * D], w_in[:, :, 3 * D + heads:]], axis=2).astype(BF16)
    w_dt_b = jnp.pad(w_in[:, :, 3 * D:3 * D + heads],
                     ((0, 0), (0, 0), (0, LANES - heads))).astype(BF16)
    w_out_b = w_out.astype(BF16)
    w_up_b = w_ffn_up.astype(BF16)
    w_down_b = w_ffn_down.astype(BF16)
    pool_w_b = pool_w.astype(BF16)

    ssd_st, conv_st, hg_st, pool_st = [None, None], [None, None], [None, None], [None, None]
    for layer in range(depth):
        j = layer // 2
        nw = norm_w[layer].astype(F32)
        if layer % 2 == 0:
            proj, dtr = _norm_mm(x, nw[0], w_main_b, w_dt_b, j, name=f"in_proj_{layer}")
            dsk = jnp.repeat(d_skip[j].astype(F32), P).reshape(1, D)
            mixed = None
            for si, seg in enumerate(segs):
                mixed, ssd_st[si], conv_st[si] = _ssd(
                    proj, dtr, ssd_in[si], conv_w[j].astype(F32),
                    conv_b[j].astype(F32).reshape(1, conv_dim), _pad_lanes(dt_bias[j]),
                    _pad_lanes(a_log[j]), dsk, ssd_norm_w[j].astype(F32).reshape(1, D),
                    (mixed, ssd_st[si], conv_st[si]), layer=j, n_layers=n_ab,
                    L=_tile(seg["T"], (128, 64, 32, 16)), D=D, G=G, P=P,
                    name=f"ssd_{layer}_{si}", **seg)
            for si, seg in enumerate(segs):
                mixed, hg_st[si] = _hgrn(
                    proj, lbs[j].reshape(1, D), hg_in[si], hg_norm_w[j].astype(F32).reshape(1, D),
                    (mixed, hg_st[si]), layer=j, n_layers=n_ab,
                    L=_tile(seg["T"], (128, 64, 32, 16)), D=D, name=f"hgrn_{layer}_{si}", **seg)
            x = _mm_res_norm(mixed, w_out_b, j, x, nw[1], name=f"out_proj_{layer}")
        else:
            for si, seg in enumerate(segs):
                x, pool_st[si] = _pool(
                    x, pool_in[si], nw[0].reshape(1, D), pool_w_b,
                    pool_scale[j].astype(F32).reshape(1, D), nw[1].reshape(1, D), pool_st[si],
                    layer=j, n_layers=n_c, L=_tile(seg["T"], (256, 128, 64, 32, 16)), D=D,
                    pos0=pos0[si], name=f"pool_{layer}_{si}", **seg)
        mlp = functools.partial(_ffn, x, nw[2], w_up_b, w_down_b, layer, nw[3])
        if layer < depth - 1:
            x = mlp(name=f"ffn_{layer}")
        else:
            y_p = mlp(row0=0, nrows=Mp, name=f"ffn_{layer}_0")
            y_s = mlp(row0=Mp, nrows=Ms, name=f"ffn_{layer}_1")

    dt_out = x_prompt.dtype
    nseqs = (B, Bs)
    outs = [y_p.reshape(B, T, D), y_s.reshape(Bs, Ts, D)]
    for si in range(2):
        outs += [conv_st[si][:, :, SUBLANES - keep:, :],
                 ssd_st[si].reshape(n_ab, nseqs[si], heads, P, SSD_STATE),
                 hg_st[si],
                 pool_st[si][:, :, 1:, :]]
    return tuple(o.astype(dt_out) for o in outs)
```
